```python
import math
import jax
import jax.numpy as jnp
from jax import lax
import numpy as np

D_MODEL = 1024
BATCH = 4
SEQ = 8192
DEPTH = 2
DEC_BATCH = 32
DEC_SEQ = 8
PAST_LEN = 16384
PAGE_SIZE = 128

RMS_EPS = 1e-5
NEG_INF = -1e30
D_RWKV = D_MODEL // 2
HD_RWKV = 64
H_RWKV = D_RWKV // HD_RWKV
DECAY_LORA = 64
ICLR_LORA = 64
GATE_LORA = 128
LN_X_EPS = 64e-5
D_POOL = D_MODEL - D_RWKV
POOL_WINDOWS = (2, 4, 8, 16)
POOL_GROUP = D_POOL // len(POOL_WINDOWS)
POOL_BUF = max(POOL_WINDOWS) - 1
N_SHIFT = 3 * D_RWKV + GATE_LORA + DECAY_LORA + ICLR_LORA
D_IN_MIX = N_SHIFT + D_POOL
HD_ATT = 64
H_ATT = D_MODEL // (2 * HD_ATT)
Q_BLOCK = 128
N_GROUPS = 4
EXPERTS_PER_GROUP = 8
N_EXPERTS = N_GROUPS * EXPERTS_PER_GROUP
TOP_K = 2
D_EXPERT = 512
MOE_BLOCK = 128
N_MIX_LAYERS = (DEPTH + 1) // 2
N_ATT_LAYERS = DEPTH // 2

kernel_name = 'rwkv7_pool_diffattn_hmoe_step'


def rms_norm(x, g):
    xf = x.astype(jnp.float32)
    y = xf * lax.rsqrt(jnp.mean(xf * xf, axis=-1, keepdims=True) + RMS_EPS)
    return (y * g.astype(jnp.float32)).astype(x.dtype)


def wkv7_scan(r, w, k, v, a, b, s0):
    def step(s, inp):
        rt, wt, kt, vt, at, bt = inp
        sa = jnp.einsum('bhvk,bhk->bhv', s, at)
        s = s * wt[:, :, None, :] + sa[..., None] * bt[:, :, None, :] + vt[..., None] * kt[:, :, None, :]
        return s, jnp.einsum('bhvk,bhk->bhv', s, rt)
    xs = tuple(jnp.swapaxes(t, 0, 1) for t in (r, w, k, v, a, b))
    s_final, ys = lax.scan(step, s0, xs)
    return jnp.swapaxes(ys, 0, 1), s_final


def rwkv_pool_mixer(h, shift0, wkv0, pool0, pos0, w_in, mu, w0, w_dec, a0, w_icl, w_gate,
                    k_k, k_a, r_k, ln_w, ln_b, pool_w, pool_scale, w_out):
    f32 = jnp.float32
    B, T, _ = h.shape
    z = h @ w_in
    zr, u = z[..., :N_SHIFT], z[..., N_SHIFT:]
    z_prev = jnp.concatenate([shift0[:, None, :].astype(z.dtype), zr[:, :-1]], axis=1)
    zs = (zr + (z_prev - zr) * mu).astype(f32)
    o1, o2, o3 = D_RWKV, 2 * D_RWKV, 3 * D_RWKV
    o4, o5 = o3 + GATE_LORA, o3 + GATE_LORA + DECAY_LORA
    r, k, v = zs[..., :o1], zs[..., o1:o2], zs[..., o2:o3]
    g_lo, d_lo, a_lo = zs[..., o3:o4], zs[..., o4:o5], zs[..., o5:]
    log_w = -jax.nn.softplus(-(w0 + jnp.tanh(d_lo) @ w_dec)) - 0.5
    decay = jnp.exp(-jnp.exp(log_w))
    iclr = jax.nn.sigmoid(a0 + a_lo @ w_icl)
    gate = jax.nn.sigmoid(g_lo) @ w_gate

    def heads(t):
        return t.reshape(B, T, H_RWKV, HD_RWKV)
    kk = heads(k * k_k)
    kk = kk * lax.rsqrt(jnp.maximum(jnp.sum(kk * kk, axis=-1, keepdims=True), 1e-24))
    k = k * (1.0 + (iclr - 1.0) * k_a)
    rh, kh, vh = heads(r), heads(k), heads(v)
    y, wkv = wkv7_scan(rh, heads(decay), kh, vh, -kk, kk * heads(iclr), wkv0.astype(f32))
    mean = jnp.mean(y, axis=-1, keepdims=True)
    var = jnp.mean(jnp.square(y - mean), axis=-1, keepdims=True)
    y = ((y - mean) * lax.rsqrt(var + LN_X_EPS)).reshape(B, T, D_RWKV) * ln_w + ln_b
    bonus = jnp.sum(rh * kh * r_k, axis=-1, keepdims=True) * vh
    y_rwkv = (y + bonus.reshape(B, T, D_RWKV)) * gate

    uf = u.astype(f32)
    ext = jnp.concatenate([pool0.astype(f32), uf], axis=1)
    cs = jnp.concatenate([jnp.zeros_like(ext[:, :1]), jnp.cumsum(ext, axis=1)], axis=1)
    pos = pos0 + jnp.arange(T)
    outs = []
    for gi, win in enumerate(POOL_WINDOWS):
        c0, c1 = gi * POOL_GROUP, (gi + 1) * POOL_GROUP
        lo = POOL_BUF + 1 - win
        window_sum = cs[:, POOL_BUF + 1:, c0:c1] - cs[:, lo:lo + T, c0:c1]
        count = jnp.minimum(pos + 1, win).astype(f32)[None, :, None]
        outs.append((window_sum / count - uf[..., c0:c1]) @ pool_w[gi])
    y_pool = jnp.concatenate(outs, axis=-1) * pool_scale
    new_pool = ext[:, -POOL_BUF:].astype(u.dtype)

    y_out = jnp.concatenate([y_rwkv, y_pool], axis=-1).astype(h.dtype) @ w_out
    return y_out, zr[:, -1], wkv, new_pool


def diff_lambda(lq1, lk1, lq2, lk2, lam_init):
    f32 = jnp.float32
    return (jnp.exp(jnp.sum(lq1.astype(f32) * lk1.astype(f32)))
            - jnp.exp(jnp.sum(lq2.astype(f32) * lk2.astype(f32))) + lam_init)


def diff_qkv(h, w_qkv):
    B, T, _ = h.shape
    q, k, v = jnp.split(h @ w_qkv, 3, axis=-1)
    shape = (B, T, H_ATT, 2 * HD_ATT)
    return q.reshape(shape), k.reshape(shape), v.reshape(shape)


def diff_attn_prompt(q, k, v, lam):
    f32 = jnp.float32
    B, T = q.shape[:2]
    nb = T // Q_BLOCK
    scale = HD_ATT ** -0.5
    kh = k.reshape(B, T, H_ATT, 2, HD_ATT)
    qb = jnp.swapaxes(q.reshape(B, nb, Q_BLOCK, H_ATT, 2, HD_ATT), 0, 1)
    kpos = jnp.arange(T)

    def block(args):
        qi, bi = args
        s = jnp.einsum('bqhcd,bkhcd->bhcqk', qi, kh, preferred_element_type=f32) * scale
        qpos = bi * Q_BLOCK + jnp.arange(Q_BLOCK)
        s = jnp.where(kpos[None, :] <= qpos[:, None], s, NEG_INF)
        p = jax.nn.softmax(s, axis=-1)
        a = (p[:, :, 0] - lam * p[:, :, 1]).astype(v.dtype)
        return jnp.einsum('bhqk,bkhe->bqhe', a, v, preferred_element_type=f32)

    o = lax.map(block, (qb, jnp.arange(nb)))
    return jnp.swapaxes(o, 0, 1).reshape(B, T, H_ATT, 2 * HD_ATT)


def diff_attn_sample(q, k_new, v_new, cache_k, cache_v, layer_j, page_table, lam):
    f32 = jnp.float32
    DB, Tn = q.shape[:2]
    past = page_table.shape[1] * PAGE_SIZE
    scale = HD_ATT ** -0.5
    kp = cache_k[layer_j, page_table].reshape(DB, past, H_ATT, 2, HD_ATT).astype(q.dtype)
    vp = cache_v[layer_j, page_table].reshape(DB, past, H_ATT, 2 * HD_ATT).astype(v_new.dtype)
    qh = q.reshape(DB, Tn, H_ATT, 2, HD_ATT)
    kn = k_new.reshape(DB, Tn, H_ATT, 2, HD_ATT)
    s_past = jnp.einsum('bqhcd,bkhcd->bhcqk', qh, kp, preferred_element_type=f32) * scale
    s_new = jnp.einsum('bqhcd,bkhcd->bhcqk', qh, kn, preferred_element_type=f32) * scale
    causal = jnp.arange(Tn)[None, :] <= jnp.arange(Tn)[:, None]
    s_new = jnp.where(causal, s_new, NEG_INF)
    p = jax.nn.softmax(jnp.concatenate([s_past, s_new], axis=-1), axis=-1)
    a = (p[:, :, 0] - lam * p[:, :, 1]).astype(v_new.dtype)
    o = jnp.einsum('bhqk,bkhe->bqhe', a[..., :past], vp, preferred_element_type=f32)
    return o + jnp.einsum('bhqk,bkhe->bqhe', a[..., past:], v_new, preferred_element_type=f32)


def diff_out(o, lam_init, sub_w, w_o, dtype):
    B, T = o.shape[:2]
    o = o * lax.rsqrt(jnp.mean(o * o, axis=-1, keepdims=True) + RMS_EPS) * sub_w.astype(jnp.float32) * (1.0 - lam_init)
    return o.reshape(B, T, D_MODEL).astype(dtype) @ w_o


def routed_experts(xf, eidx, gates, w_gate, w_up, w_down):
    n_tok, d = xf.shape
    n_asg = n_tok * TOP_K
    e_flat = eidx.reshape(n_asg)
    order = jnp.argsort(e_flat)
    e_sorted = e_flat[order]
    tok_sorted = order // TOP_K
    counts = jnp.bincount(e_flat, length=N_EXPERTS)
    padded = (counts + MOE_BLOCK - 1) // MOE_BLOCK * MOE_BLOCK
    pad_end = jnp.cumsum(padded)
    pad_start = pad_end - padded
    seg_start = jnp.cumsum(counts) - counts
    dest = pad_start[e_sorted] + jnp.arange(n_asg) - seg_start[e_sorted]
    n_blocks = -(-n_asg // MOE_BLOCK) + N_EXPERTS
    buf = jnp.zeros((n_blocks * MOE_BLOCK, d), xf.dtype).at[dest].set(xf[tok_sorted])
    block_expert = jnp.minimum(
        jnp.searchsorted(pad_end, jnp.arange(n_blocks) * MOE_BLOCK, side='right'), N_EXPERTS - 1)

    def expert_block(args):
        xb, e = args
        hdn = jax.nn.silu(xb @ w_gate[e]) * (xb @ w_up[e])
        return hdn @ w_down[e]

    yb = lax.map(expert_block, (buf.reshape(n_blocks, MOE_BLOCK, d), block_expert))
    y_sorted = yb.reshape(n_blocks * MOE_BLOCK, d)[dest] * gates.reshape(n_asg)[order][:, None]
    return jax.ops.segment_sum(y_sorted, tok_sorted, num_segments=n_tok)


def hier_moe(h, w_rg, b_rg, w_re, b_re, w_gate, w_up, w_down):
    f32 = jnp.float32
    B, T, d = h.shape
    xf = h.reshape(B * T, d)
    lg = jnp.matmul(xf, w_rg, preferred_element_type=f32) + b_rg
    p_group = jax.nn.softmax(lg, axis=-1)
    grp = jnp.argmax(lg, axis=-1)
    g_gate = jnp.take_along_axis(p_group, grp[:, None], axis=-1)
    le = (jnp.matmul(xf, w_re, preferred_element_type=f32) + b_re).reshape(B * T, N_GROUPS, EXPERTS_PER_GROUP)
    le_g = jnp.take_along_axis(le, grp[:, None, None], axis=1)[:, 0]
    top_val, top_idx = lax.top_k(le_g, TOP_K)
    gates = (g_gate * jax.nn.softmax(top_val, axis=-1)).astype(h.dtype)
    eidx = grp[:, None] * EXPERTS_PER_GROUP + top_idx
    return routed_experts(xf, eidx, gates, w_gate, w_up, w_down).reshape(B, T, d)


def setup_inputs(seed: int = 0) -> dict:
    key = jax.random.key(seed)
    keys = jax.random.split(key, 48)
    f32 = jnp.float32

    def nrm(i, shape, scale):
        return jax.random.normal(keys[i], shape, f32) * scale

    D = D_MODEL
    NM, NA = N_MIX_LAYERS, N_ATT_LAYERS
    n_pages = PAST_LEN // PAGE_SIZE
    n_phys = (DEC_BATCH * n_pages * 5) // 4
    page_table = jax.random.permutation(keys[7], n_phys)[:DEC_BATCH * n_pages].reshape(DEC_BATCH, n_pages).astype(jnp.int32)
    decay_base = jnp.linspace(-6.5, -1.5, D_RWKV, dtype=f32)
    return {
        'x_prompt': nrm(0, (BATCH, SEQ, D), 1.0),
        'x_sample': nrm(1, (DEC_BATCH, DEC_SEQ, D), 1.0),
        'state_wkv': nrm(2, (NM, DEC_BATCH, H_RWKV, HD_RWKV, HD_RWKV), 0.3),
        'state_shift': nrm(3, (NM, DEC_BATCH, N_SHIFT), 1.0),
        'state_pool': nrm(4, (NM, DEC_BATCH, POOL_BUF, D_POOL), 1.0),
        'cache_k': nrm(5, (NA, n_phys, PAGE_SIZE, H_ATT, 2 * HD_ATT), 1.0),
        'cache_v': nrm(6, (NA, n_phys, PAGE_SIZE, H_ATT, 2 * HD_ATT), 1.0),
        'page_table': page_table,
        'norm_mix': 1.0 + nrm(8, (DEPTH, D), 0.02),
        'norm_ffn': 1.0 + nrm(9, (DEPTH, D), 0.02),
        'norm_final': 1.0 + nrm(10, (D,), 0.02),
        'w_in_mix': nrm(11, (NM, D, D_IN_MIX), D ** -0.5),
        'mu_shift': jax.random.uniform(keys[12], (NM, N_SHIFT), f32),
        'w_decay0': decay_base + nrm(13, (NM, D_RWKV), 0.1),
        'w_decay_up': nrm(14, (NM, DECAY_LORA, D_RWKV), 0.1),
        'w_iclr0': nrm(15, (NM, D_RWKV), 0.1),
        'w_iclr_up': nrm(16, (NM, ICLR_LORA, D_RWKV), 0.1),
        'w_gate_up': nrm(17, (NM, GATE_LORA, D_RWKV), GATE_LORA ** -0.5),
        'k_k': 0.85 + nrm(18, (NM, D_RWKV), 0.02),
        'k_a': 1.0 + nrm(19, (NM, D_RWKV), 0.02),
        'r_k': nrm(20, (NM, H_RWKV, HD_RWKV), 0.1),
        'ln_x_w': 1.0 + nrm(21, (NM, D_RWKV), 0.02),
        'ln_x_b': nrm(22, (NM, D_RWKV), 0.02),
        'pool_w': nrm(23, (NM, len(POOL_WINDOWS), POOL_GROUP, POOL_GROUP), POOL_GROUP ** -0.5),
        'pool_scale': 1.0 + nrm(24, (NM, D_POOL), 0.1),
        'w_out_mix': nrm(25, (NM, D, D), D ** -0.5),
        'w_qkv': nrm(26, (NA, D, 3 * D), D ** -0.5),
        'lam_q1': nrm(27, (NA, HD_ATT), 0.1),
        'lam_k1': nrm(28, (NA, HD_ATT), 0.1),
        'lam_q2': nrm(29, (NA, HD_ATT), 0.1),
        'lam_k2': nrm(30, (NA, HD_ATT), 0.1),
        'subln_w': 1.0 + nrm(31, (NA, 2 * HD_ATT), 0.02),
        'w_o': nrm(32, (NA, D, D), D ** -0.5),
        'router_group': nrm(33, (DEPTH, D, N_GROUPS), D ** -0.5),
        'router_group_bias': nrm(34, (DEPTH, N_GROUPS), 0.01),
        'router_expert': nrm(35, (DEPTH, D, N_EXPERTS), D ** -0.5),
        'router_expert_bias': nrm(36, (DEPTH, N_EXPERTS), 0.01),
        'w_exp_gate': nrm(37, (DEPTH, N_EXPERTS, D, D_EXPERT), D ** -0.5),
        'w_exp_up': nrm(38, (DEPTH, N_EXPERTS, D, D_EXPERT), D ** -0.5),
        'w_exp_down': nrm(39, (DEPTH, N_EXPERTS, D_EXPERT, D), D_EXPERT ** -0.5),
    }


def reference(x_prompt, x_sample, state_wkv, state_shift, state_pool, cache_k, cache_v, page_table,
              norm_mix, norm_ffn, norm_final,
              w_in_mix, mu_shift, w_decay0, w_decay_up, w_iclr0, w_iclr_up, w_gate_up,
              k_k, k_a, r_k, ln_x_w, ln_x_b, pool_w, pool_scale, w_out_mix,
              w_qkv, lam_q1, lam_k1, lam_q2, lam_k2, subln_w, w_o,
              router_group, router_group_bias, router_expert, router_expert_bias,
              w_exp_gate, w_exp_up, w_exp_down):
    xp, xs = x_prompt, x_sample
    bp = xp.shape[0]
    wkv_p, shift_p, pool_p, k_p, v_p = [], [], [], [], []
    wkv_s, shift_s, pool_s, k_s, v_s = [], [], [], [], []
    for layer in range(DEPTH):
        j = layer // 2
        hp = rms_norm(xp, norm_mix[layer])
        hs = rms_norm(xs, norm_mix[layer])
        if layer % 2 == 0:
            mix = (w_in_mix[j], mu_shift[j], w_decay0[j], w_decay_up[j], w_iclr0[j], w_iclr_up[j],
                   w_gate_up[j], k_k[j], k_a[j], r_k[j], ln_x_w[j], ln_x_b[j], pool_w[j], pool_scale[j],
                   w_out_mix[j])
            yp, sh_p, wk_p, pl_p = rwkv_pool_mixer(
                hp, jnp.zeros((bp, N_SHIFT), hp.dtype),
                jnp.zeros((bp, H_RWKV, HD_RWKV, HD_RWKV), jnp.float32),
                jnp.zeros((bp, POOL_BUF, D_POOL), hp.dtype), 0, *mix)
            ys, sh_s, wk_s, pl_s = rwkv_pool_mixer(
                hs, state_shift[j], state_wkv[j], state_pool[j], PAST_LEN, *mix)
            wkv_p.append(wk_p); shift_p.append(sh_p); pool_p.append(pl_p)
            wkv_s.append(wk_s); shift_s.append(sh_s); pool_s.append(pl_s)
        else:
            lam_init = 0.8 - 0.6 * math.exp(-0.3 * layer)
            lam = diff_lambda(lam_q1[j], lam_k1[j], lam_q2[j], lam_k2[j], lam_init)
            qp, kp_, vp_ = diff_qkv(hp, w_qkv[j])
            qs, ks_, vs_ = diff_qkv(hs, w_qkv[j])
            op = diff_attn_prompt(qp, kp_, vp_, lam)
            os_ = diff_attn_sample(qs, ks_, vs_, cache_k, cache_v, j, page_table, lam)
            yp = diff_out(op, lam_init, subln_w[j], w_o[j], hp.dtype)
            ys = diff_out(os_, lam_init, subln_w[j], w_o[j], hs.dtype)
            k_p.append(kp_); v_p.append(vp_)
            k_s.append(ks_); v_s.append(vs_)
        xp = xp + yp
        xs = xs + ys
        moe = (router_group[layer], router_group_bias[layer], router_expert[layer],
               router_expert_bias[layer], w_exp_gate[layer], w_exp_up[layer], w_exp_down[layer])
        xp = xp + hier_moe(rms_norm(xp, norm_ffn[layer]), *moe)
        xs = xs + hier_moe(rms_norm(xs, norm_ffn[layer]), *moe)
    y_prompt = rms_norm(xp, norm_final)
    y_sample = rms_norm(xs, norm_final)
    wkv_prompt = jnp.stack(wkv_p)
    shift_prompt = jnp.stack(shift_p)
    pool_prompt = jnp.stack(pool_p)
    k_prompt = jnp.stack(k_p)
    v_prompt = jnp.stack(v_p)
    wkv_sample = jnp.stack(wkv_s)
    shift_sample = jnp.stack(shift_s)
    pool_sample = jnp.stack(pool_s)
    k_sample = jnp.stack(k_s)
    v_sample = jnp.stack(v_s)
    return (y_prompt, y_sample, wkv_prompt, shift_prompt, pool_prompt, k_prompt, v_prompt,
            wkv_sample, shift_sample, pool_sample, k_sample, v_sample)
```

```python
import functools
import math

import jax
import jax.numpy as jnp
from jax import lax
from jax.experimental import pallas as pl
from jax.experimental.pallas import tpu as pltpu

F32 = jnp.float32
BF16 = jnp.bfloat16

V7X_LANES = 128
V7X_SUBLANES = 8
V7X_VMEM_LIMIT_BYTES = 56 * 1024 * 1024

D_MODEL = 1024
RMS_EPS = 1e-5
NEG_INF = -1e30
D_RWKV = 512
HD_RWKV = 64
H_RWKV = 8
N_PAIRS = H_RWKV // 2
DECAY_LORA = 64
ICLR_LORA = 64
GATE_LORA = 128
LN_X_EPS = 64e-5
D_POOL = 512
POOL_WINDOWS = (2, 4, 8, 16)
POOL_GROUP = 128
POOL_BUF = 15
POOL_CARRY = 16
N_SHIFT = 3 * D_RWKV + GATE_LORA + DECAY_LORA + ICLR_LORA
HD_ATT = 64
H_ATT = 8
D_HEAD = 2 * HD_ATT
PAGE_SIZE = 128
N_GROUPS = 4
EXPERTS_PER_GROUP = 8
N_EXPERTS = 32
TOP_K = 2
D_EXPERT = 512
WKV_CHUNK = 64
MOE_ROWS = 256
PAGES_PER_STEP = 8


def _params(*semantics):
    return pltpu.CompilerParams(dimension_semantics=semantics, vmem_limit_bytes=V7X_VMEM_LIMIT_BYTES)


def _bdot(a, b):
    return jnp.dot(a.astype(BF16), b.astype(BF16), preferred_element_type=F32)


def _bdot_nt(a, b):
    return lax.dot_general(a.astype(BF16), b.astype(BF16), (((1,), (1,)), ((), ())),
                           preferred_element_type=F32)


def _bdot_tn(a, b):
    return lax.dot_general(a.astype(BF16), b.astype(BF16), (((0,), (0,)), ((), ())),
                           preferred_element_type=F32)


def _split3(x):
    hi = x.astype(BF16)
    r1 = x - hi.astype(F32)
    mid = r1.astype(BF16)
    lo = (r1 - mid.astype(F32)).astype(BF16)
    return hi, mid, lo


def _sigmoid(x):
    return 1.0 / (1.0 + jnp.exp(-x))


def _softplus(x):
    return jnp.maximum(x, 0.0) + jnp.log(1.0 + jnp.exp(-jnp.abs(x)))


def _rms(x, g):
    return x * lax.rsqrt(jnp.mean(x * x, axis=-1, keepdims=True) + RMS_EPS) * g


def _norm_matmul_kernel(x_ref, g_ref, w_ref, *out_refs):
    h = _rms(x_ref[...], g_ref[...]).astype(BF16)
    off = 0
    for o_ref in out_refs:
        n = o_ref.shape[-1]
        o_ref[...] = jnp.dot(h, w_ref[:, off:off + n], preferred_element_type=F32)
        off += n


def _norm_matmul(x, g, w_bf16, splits):
    n, d = x.shape
    tm = min(512, n)
    assert n % tm == 0 and sum(splits) == w_bf16.shape[1]
    return pl.pallas_call(
        _norm_matmul_kernel,
        grid=(n // tm,),
        in_specs=[pl.BlockSpec((tm, d), lambda i: (i, 0)),
                  pl.BlockSpec((1, d), lambda i: (0, 0)),
                  pl.BlockSpec(w_bf16.shape, lambda i: (0, 0))],
        out_specs=[pl.BlockSpec((tm, s), lambda i: (i, 0)) for s in splits],
        out_shape=[jax.ShapeDtypeStruct((n, s), F32) for s in splits],
        compiler_params=_params("parallel"),
        name="norm_matmul",
    )(x, g.reshape(1, d), w_bf16)


def _matmul_res_kernel(res_ref, g_ref, *refs):
    xo_ref, ho_ref = refs[-2:]
    acc = res_ref[...]
    for a_ref, w_ref in zip(refs[0:-2:2], refs[1:-2:2]):
        acc = acc + jnp.dot(a_ref[...].astype(BF16), w_ref[...], preferred_element_type=F32)
    xo_ref[...] = acc
    ho_ref[...] = _rms(acc, g_ref[...])


def _matmul_res(res, pairs, g):
    n, d = res.shape
    tm = min(512, n)
    assert n % tm == 0
    in_specs = [pl.BlockSpec((tm, d), lambda i: (i, 0)), pl.BlockSpec((1, d), lambda i: (0, 0))]
    args = [res, g.reshape(1, d)]
    for a, w in pairs:
        in_specs += [pl.BlockSpec((tm, a.shape[1]), lambda i: (i, 0)), pl.BlockSpec(w.shape, lambda i: (0, 0))]
        args += [a, w]
    return pl.pallas_call(
        _matmul_res_kernel,
        grid=(n // tm,),
        in_specs=in_specs,
        out_specs=[pl.BlockSpec((tm, d), lambda i: (i, 0))] * 2,
        out_shape=[jax.ShapeDtypeStruct((n, d), F32)] * 2,
        compiler_params=_params("parallel"),
        name="matmul_res",
    )(*args)


def _combine_kernel(x_ref, y_ref, g_ref, o_ref, *, normed):
    x = x_ref[...] + y_ref[:, 0, :] + y_ref[:, 1, :]
    o_ref[...] = _rms(x, g_ref[...]) if normed else x


def _combine(x, y2, g, normed):
    n, d = x.shape
    tm = min(256, n)
    assert n % tm == 0
    return pl.pallas_call(
        functools.partial(_combine_kernel, normed=normed),
        grid=(n // tm,),
        in_specs=[pl.BlockSpec((tm, d), lambda i: (i, 0)),
                  pl.BlockSpec((tm, TOP_K, d), lambda i: (i, 0, 0)),
                  pl.BlockSpec((1, d), lambda i: (0, 0))],
        out_specs=pl.BlockSpec((tm, d), lambda i: (i, 0)),
        out_shape=jax.ShapeDtypeStruct((n, d), F32),
        compiler_params=_params("parallel"),
        name="combine",
    )(x, y2, g.reshape(1, d))


def _block_diag(x, left):
    return jnp.concatenate([jnp.where(left, x, 0.0), jnp.where(left, 0.0, x)], axis=0)


def _seg_sum(x, e_ref):
    hi, mid, lo = _split3(x)
    e = e_ref[...]
    return (jnp.dot(hi, e, preferred_element_type=F32) + jnp.dot(mid, e, preferred_element_type=F32)
            + jnp.dot(lo, e, preferred_element_type=F32))


def _rwkv_kernel(zr_ref, shift0_ref, s0_ref, mu_ref, w0_ref, wcomb_ref, a0_ref, wgate_ref, kk_ref, ka_ref,
                 rk_ref, lnw_ref, lnb_ref, e_ref, y_ref, sout_ref, s_scr, prev_scr, *, t_valid):
    c_len = WKV_CHUNK
    c = pl.program_id(1)

    @pl.when(c == 0)
    def _():
        s_scr[...] = s0_ref[0]
        prev_scr[...] = shift0_ref[0]

    zr = zr_ref[...]
    row = lax.broadcasted_iota(jnp.int32, (c_len, 1), 0)
    lane = lax.broadcasted_iota(jnp.int32, (1, V7X_LANES), 1)
    left = lane < HD_RWKV
    z_prev = jnp.where(row == 0, prev_scr[...], pltpu.roll(zr, 1, 0))
    prev_scr[...] = zr[c_len - 1:c_len, :]
    zs = zr + (z_prev - zr) * mu_ref[...]

    o3 = 3 * D_RWKV
    g_lo = zs[:, o3:o3 + GATE_LORA]
    da = zs[:, o3 + GATE_LORA:]
    da = jnp.where(left, jnp.tanh(da), da)
    pre = _bdot(da, wcomb_ref[...])
    gate = _bdot(_sigmoid(g_lo), wgate_ref[...])

    valid = row < t_valid
    s_le_t = (lane % HD_RWKV) <= row
    s_lt_t = (lane % HD_RWKV) < row
    same_head = (lax.broadcasted_iota(jnp.int32, (V7X_LANES, 1), 0) < HD_RWKV) == left
    n_doublings = int(math.log2(c_len))

    for p in range(N_PAIRS):
        sl = slice(p * V7X_LANES, (p + 1) * V7X_LANES)
        r = zs[:, sl]
        k = zs[:, D_RWKV + p * V7X_LANES:D_RWKV + (p + 1) * V7X_LANES]
        v = zs[:, 2 * D_RWKV + p * V7X_LANES:2 * D_RWKV + (p + 1) * V7X_LANES]
        lw = -jnp.exp(-_softplus(-(w0_ref[:, sl] + pre[:, sl])) - 0.5)
        iclr = _sigmoid(a0_ref[:, sl] + pre[:, D_RWKV + p * V7X_LANES:D_RWKV + (p + 1) * V7X_LANES])
        kk = k * kk_ref[:, sl]
        kk = kk * lax.rsqrt(jnp.maximum(_seg_sum(kk * kk, e_ref), 1e-24))
        k = k * (1.0 + (iclr - 1.0) * ka_ref[:, sl])
        a = -kk
        b = kk * iclr
        if t_valid < c_len:
            lw = jnp.where(valid, lw, 0.0)
            a = jnp.where(valid, a, 0.0)
            b = jnp.where(valid, b, 0.0)
            k = jnp.where(valid, k, 0.0)
            v = jnp.where(valid, v, 0.0)

        g = lw
        sh = 1
        while sh < c_len:
            g = g + jnp.where(row >= sh, pltpu.roll(g, sh, 0), 0.0)
            sh *= 2
        e_g = jnp.exp(g)
        e_ng = jnp.exp(-g)
        g_end = e_g[c_len - 1:c_len, :]
        a_t = a * jnp.exp(g - lw)
        b_t = b * e_ng
        k_t = k * e_ng
        r_t = r * e_g

        s = s_scr[p]
        bd_v = _block_diag(v, left)
        sc = _bdot_nt(jnp.concatenate([a_t, r_t], axis=0),
                      jnp.concatenate([_block_diag(b_t, left), _block_diag(k_t, left)], axis=0))
        a_ab = jnp.where(s_lt_t, sc[:c_len, :V7X_LANES], 0.0)
        a_ak = jnp.where(s_lt_t, sc[:c_len, V7X_LANES:], 0.0)
        a_rb = jnp.where(s_le_t, sc[c_len:, :V7X_LANES], 0.0)
        a_rk = jnp.where(s_le_t, sc[c_len:, V7X_LANES:], 0.0)

        x = _bdot_nt(a_t, s) + _bdot(a_ak, bd_v)
        lmat = a_ab
        for i in range(n_doublings):
            x = x + _bdot(lmat, _block_diag(x, left))
            if i + 1 < n_doublings:
                lmat = _bdot(lmat, _block_diag(lmat, left))
        u = x
        y = _bdot_nt(r_t, s) + _bdot(a_rb, _block_diag(u, left)) + _bdot(a_rk, bd_v)
        s_new = s * g_end + _bdot_tn(jnp.concatenate([u, v], axis=0),
                                     jnp.concatenate([b_t * g_end, k_t * g_end], axis=0))
        s_scr[p] = jnp.where(same_head, s_new, 0.0)

        mean = _seg_sum(y, e_ref) * (1.0 / HD_RWKV)
        d = y - mean
        var = _seg_sum(d * d, e_ref) * (1.0 / HD_RWKV)
        yn = d * lax.rsqrt(var + LN_X_EPS) * lnw_ref[:, sl] + lnb_ref[:, sl]
        bonus = _seg_sum(r * k * rk_ref[:, sl], e_ref) * v
        y_ref[:, sl] = (yn + bonus) * gate[:, sl]

    sout_ref[0] = s_scr[...]


def _rwkv_mix(zr, n_seq, t_pad, t_valid, shift0, s0_bd, mix):
    n_chunks = t_pad // WKV_CHUNK
    assert t_pad % WKV_CHUNK == 0 and (t_valid == WKV_CHUNK or n_chunks == 1)
    rowvec = lambda n: pl.BlockSpec((1, n), lambda s, c: (0, 0))
    full = lambda a: pl.BlockSpec(a.shape, lambda s, c: (0,) * a.ndim)
    e128 = jnp.kron(jnp.eye(2, dtype=F32), jnp.ones((HD_RWKV, HD_RWKV), F32)).astype(BF16)
    return pl.pallas_call(
        functools.partial(_rwkv_kernel, t_valid=t_valid),
        grid=(n_seq, n_chunks),
        in_specs=[pl.BlockSpec((WKV_CHUNK, N_SHIFT), lambda s, c: (s * n_chunks + c, 0)),
                  pl.BlockSpec((1, 1, N_SHIFT), lambda s, c: (s, 0, 0)),
                  pl.BlockSpec((1, N_PAIRS, V7X_LANES, V7X_LANES), lambda s, c: (s, 0, 0, 0)),
                  rowvec(N_SHIFT), rowvec(D_RWKV), full(mix["wcomb"]), rowvec(D_RWKV), full(mix["wgate"]),
                  rowvec(D_RWKV), rowvec(D_RWKV), rowvec(D_RWKV), rowvec(D_RWKV), rowvec(D_RWKV), full(e128)],
        out_specs=[pl.BlockSpec((WKV_CHUNK, D_RWKV), lambda s, c: (s * n_chunks + c, 0)),
                   pl.BlockSpec((1, N_PAIRS, V7X_LANES, V7X_LANES), lambda s, c: (s, 0, 0, 0))],
        out_shape=[jax.ShapeDtypeStruct((n_seq * t_pad, D_RWKV), F32),
                   jax.ShapeDtypeStruct((n_seq, N_PAIRS, V7X_LANES, V7X_LANES), F32)],
        scratch_shapes=[pltpu.VMEM((N_PAIRS, V7X_LANES, V7X_LANES), F32), pltpu.VMEM((1, N_SHIFT), F32)],
        compiler_params=_params("parallel", "arbitrary"),
        name="rwkv_mix",
    )(zr, shift0.reshape(n_seq, 1, N_SHIFT), s0_bd, mix["mu"], mix["w0"], mix["wcomb"], mix["a0"], mix["wgate"],
      mix["k_k"], mix["k_a"], mix["r_k"], mix["ln_w"], mix["ln_b"], e128)


def _state_to_block_diag(s):
    n = s.shape[0]
    s = s.reshape(n, N_PAIRS, 2, HD_RWKV, HD_RWKV)
    eye = jnp.eye(2, dtype=s.dtype)
    bd = jnp.einsum("nphvk,hg->nphvgk", s, eye)
    return bd.reshape(n, N_PAIRS, V7X_LANES, V7X_LANES)


def _state_from_block_diag(bd):
    n = bd.shape[0]
    bd = bd.reshape(n, N_PAIRS, 2, HD_RWKV, 2, HD_RWKV)
    s = jnp.stack([bd[:, :, 0, :, 0, :], bd[:, :, 1, :, 1, :]], axis=2)
    return s.reshape(n, H_RWKV, HD_RWKV, HD_RWKV)


def _pool_kernel(u_ref, hist_ref, w_ref, scale_ref, y_ref, carry_scr, *, pos0):
    t = pl.program_id(1)
    tm = u_ref.shape[0]

    @pl.when(t == 0)
    def _():
        carry_scr[...] = hist_ref[0]

    u = u_ref[...]
    ext = jnp.concatenate([carry_scr[...], u], axis=0)
    carry_scr[...] = ext[tm:, :]
    pos = pos0 + t * tm + lax.broadcasted_iota(jnp.int32, (tm, 1), 0)
    acc = ext
    sums = []
    for win in POOL_WINDOWS:
        acc = acc + pltpu.roll(acc, win // 2, 0)
        sums.append(acc[POOL_CARRY:, :])
    for gi, win in enumerate(POOL_WINDOWS):
        sl = slice(gi * POOL_GROUP, (gi + 1) * POOL_GROUP)
        count = jnp.minimum(pos + 1, win).astype(F32)
        x = sums[gi][:, sl] / count - u[:, sl]
        y_ref[:, sl] = _bdot(x, w_ref[gi]) * scale_ref[:, sl]


def _pool_mix(u, n_seq, t_len, hist, pos0, pool_w_bf16, pool_scale):
    tm = min(512, t_len)
    assert t_len % tm == 0
    n_t = t_len // tm
    return pl.pallas_call(
        functools.partial(_pool_kernel, pos0=pos0),
        grid=(n_seq, n_t),
        in_specs=[pl.BlockSpec((tm, D_POOL), lambda s, t: (s * n_t + t, 0)),
                  pl.BlockSpec((1, POOL_CARRY, D_POOL), lambda s, t: (s, 0, 0)),
                  pl.BlockSpec(pool_w_bf16.shape, lambda s, t: (0, 0, 0)),
                  pl.BlockSpec((1, D_POOL), lambda s, t: (0, 0))],
        out_specs=pl.BlockSpec((tm, D_POOL), lambda s, t: (s * n_t + t, 0)),
        out_shape=jax.ShapeDtypeStruct((n_seq * t_len, D_POOL), F32),
        scratch_shapes=[pltpu.VMEM((POOL_CARRY, D_POOL), F32)],
        compiler_params=_params("parallel", "arbitrary"),
        name="pool_mix",
    )(u, hist, pool_w_bf16, pool_scale.reshape(1, D_POOL))


def _sub_norm(acc, l, lam, subw, lam_init):
    tq = acc.shape[0] // 2
    o = acc[:tq] / l[:tq] - lam * (acc[tq:] / l[tq:])
    return o * lax.rsqrt(jnp.mean(o * o, axis=-1, keepdims=True) + RMS_EPS) * subw * (1.0 - lam_init)


def _attn_prompt_kernel(lam_ref, q_ref, k_ref, v_ref, subw_ref, o_ref, m_scr, l_scr, acc_scr, *, lam_init):
    tq = q_ref.shape[1]
    qi = pl.program_id(2)
    lane = lax.broadcasted_iota(jnp.int32, (1, D_HEAD), 1)
    q = q_ref[0] * (HD_ATT ** -0.5)
    qs = jnp.concatenate([jnp.where(lane < HD_ATT, q, 0.0), jnp.where(lane < HD_ATT, 0.0, q)], axis=0).astype(BF16)
    m_scr[...] = jnp.full(m_scr.shape, NEG_INF, F32)
    l_scr[...] = jnp.zeros(l_scr.shape, F32)
    acc_scr[...] = jnp.zeros(acc_scr.shape, F32)

    def block(j, masked):
        kb = k_ref[0, pl.ds(j * tq, tq), :].astype(BF16)
        vb = v_ref[0, pl.ds(j * tq, tq), :].astype(BF16)
        s = lax.dot_general(qs, kb, (((1,), (1,)), ((), ())), preferred_element_type=F32)
        if masked:
            qpos = lax.broadcasted_iota(jnp.int32, (2 * tq, 1), 0) % tq
            kpos = lax.broadcasted_iota(jnp.int32, (1, tq), 1)
            s = jnp.where(kpos <= qpos, s, NEG_INF)
        m_old = m_scr[...]
        m_new = jnp.maximum(m_old, jnp.max(s, axis=-1, keepdims=True))
        alpha = jnp.exp(m_old - m_new)
        p = jnp.exp(s - m_new)
        l_scr[...] = alpha * l_scr[...] + jnp.sum(p, axis=-1, keepdims=True)
        acc_scr[...] = alpha * acc_scr[...] + jnp.dot(p.astype(BF16), vb, preferred_element_type=F32)
        m_scr[...] = m_new

    def body(j, carry):
        block(j, False)
        return carry

    lax.fori_loop(0, qi, body, 0)
    block(qi, True)
    o_ref[0] = _sub_norm(acc_scr[...], l_scr[...], lam_ref[0, 0], subw_ref[...], lam_init)


def _attn_prompt(q, k, v, lam, subw, lam_init, n_batch, t_len):
    tq = min(256, t_len)
    assert t_len % tq == 0
    kv_spec = pl.BlockSpec((1, t_len, D_HEAD), lambda b, h, i: (b, 0, h))
    return pl.pallas_call(
        functools.partial(_attn_prompt_kernel, lam_init=lam_init),
        grid=(n_batch, H_ATT, t_len // tq),
        in_specs=[pl.BlockSpec(memory_space=pltpu.SMEM),
                  pl.BlockSpec((1, tq, D_HEAD), lambda b, h, i: (b, i, h)),
                  kv_spec, kv_spec,
                  pl.BlockSpec((1, D_HEAD), lambda b, h, i: (0, 0))],
        out_specs=pl.BlockSpec((1, tq, D_HEAD), lambda b, h, i: (b, i, h)),
        out_shape=jax.ShapeDtypeStruct((n_batch, t_len, D_MODEL), F32),
        scratch_shapes=[pltpu.VMEM((2 * tq, 1), F32), pltpu.VMEM((2 * tq, 1), F32),
                        pltpu.VMEM((2 * tq, D_HEAD), F32)],
        compiler_params=_params("parallel", "parallel", "arbitrary"),
        name="attn_prompt",
    )(lam.reshape(1, 1), q, k, v, subw.reshape(1, D_HEAD))


def _attn_sample_kernel(pt_ref, lam_ref, q_ref, kn_ref, vn_ref, subw_ref, *refs, lam_init, t_new):
    k_refs = refs[:PAGES_PER_STEP]
    v_refs = refs[PAGES_PER_STEP:2 * PAGES_PER_STEP]
    o_ref, qb_scr, kc_scr, vc_scr, m_scr, l_scr, acc_scr = refs[2 * PAGES_PER_STEP:]
    g = pl.program_id(1)
    n_rows = 2 * H_ATT * t_new
    rblk = lax.broadcasted_iota(jnp.int32, (n_rows, 1), 0) // t_new
    cblk = lax.broadcasted_iota(jnp.int32, (1, D_MODEL), 1) // HD_ATT

    @pl.when(g == 0)
    def _():
        q = q_ref[0] * (HD_ATT ** -0.5)
        qt = jnp.concatenate([q] * (2 * H_ATT), axis=0)
        qb_scr[...] = jnp.where(rblk == cblk, qt, 0.0).astype(BF16)
        m_scr[...] = jnp.full(m_scr.shape, NEG_INF, F32)
        l_scr[...] = jnp.zeros(l_scr.shape, F32)
        acc_scr[...] = jnp.zeros(acc_scr.shape, F32)

    def update(s, vmat):
        m_old = m_scr[...]
        m_new = jnp.maximum(m_old, jnp.max(s, axis=-1, keepdims=True))
        alpha = jnp.exp(m_old - m_new)
        p = jnp.exp(s - m_new)
        l_scr[...] = alpha * l_scr[...] + jnp.sum(p, axis=-1, keepdims=True)
        acc_scr[...] = alpha * acc_scr[...] + jnp.dot(p.astype(BF16), vmat, preferred_element_type=F32)
        m_scr[...] = m_new

    for i in range(PAGES_PER_STEP):
        kc_scr[i * PAGE_SIZE:(i + 1) * PAGE_SIZE, :] = k_refs[i][0, 0].astype(BF16)
        vc_scr[i * PAGE_SIZE:(i + 1) * PAGE_SIZE, :] = v_refs[i][0, 0].astype(BF16)
    s = lax.dot_general(qb_scr[...], kc_scr[...], (((1,), (1,)), ((), ())), preferred_element_type=F32)
    update(s, vc_scr[...])

    @pl.when(g == pl.num_programs(1) - 1)
    def _():
        pad = jnp.zeros((PAGE_SIZE - t_new, D_MODEL), BF16)
        kn = jnp.concatenate([kn_ref[0].astype(BF16), pad], axis=0)
        vn = jnp.concatenate([vn_ref[0].astype(BF16), pad], axis=0)
        s_new = lax.dot_general(qb_scr[...], kn, (((1,), (1,)), ((), ())), preferred_element_type=F32)
        qpos = lax.broadcasted_iota(jnp.int32, (n_rows, 1), 0) % t_new
        kpos = lax.broadcasted_iota(jnp.int32, (1, PAGE_SIZE), 1)
        update(jnp.where(kpos <= qpos, s_new, NEG_INF), vn)
        acc = acc_scr[...]
        l = l_scr[...]
        lam = lam_ref[0, 0]
        for h in range(H_ATT):
            r0 = 2 * h * t_new
            a0 = acc[r0:r0 + t_new, h * D_HEAD:(h + 1) * D_HEAD]
            a1 = acc[r0 + t_new:r0 + 2 * t_new, h * D_HEAD:(h + 1) * D_HEAD]
            stacked = jnp.concatenate([a0, a1], axis=0)
            ls = jnp.concatenate([l[r0:r0 + t_new], l[r0 + t_new:r0 + 2 * t_new]], axis=0)
            o_ref[0, :, h * D_HEAD:(h + 1) * D_HEAD] = _sub_norm(stacked, ls, lam, subw_ref[...], lam_init)


def _attn_sample(q, k_new, v_new, cache_k, cache_v, layer_j, page_table, lam, subw, lam_init):
    n_batch, t_new, _ = q.shape
    n_pages = page_table.shape[1]
    assert n_pages % PAGES_PER_STEP == 0 and t_new <= PAGE_SIZE and t_new % V7X_SUBLANES == 0
    n_steps = n_pages // PAGES_PER_STEP
    n_layers, n_phys = cache_k.shape[:2]
    ck = cache_k.reshape(n_layers, n_phys, PAGE_SIZE, D_MODEL)
    cv = cache_v.reshape(n_layers, n_phys, PAGE_SIZE, D_MODEL)
    n_rows = 2 * H_ATT * t_new

    def page_spec(i):
        return pl.BlockSpec((1, 1, PAGE_SIZE, D_MODEL),
                            lambda b, g, pt: (layer_j, pt[b, g * PAGES_PER_STEP + i], 0, 0))

    tok_spec = pl.BlockSpec((1, t_new, D_MODEL), lambda b, g, pt: (b, 0, 0))
    grid_spec = pltpu.PrefetchScalarGridSpec(
        num_scalar_prefetch=1,
        grid=(n_batch, n_steps),
        in_specs=[pl.BlockSpec(memory_space=pltpu.SMEM), tok_spec, tok_spec, tok_spec,
                  pl.BlockSpec((1, D_HEAD), lambda b, g, pt: (0, 0))]
                 + [page_spec(i) for i in range(PAGES_PER_STEP)] * 2,
        out_specs=tok_spec,
        scratch_shapes=[pltpu.VMEM((n_rows, D_MODEL), BF16),
                        pltpu.VMEM((PAGES_PER_STEP * PAGE_SIZE, D_MODEL), BF16),
                        pltpu.VMEM((PAGES_PER_STEP * PAGE_SIZE, D_MODEL), BF16),
                        pltpu.VMEM((n_rows, 1), F32), pltpu.VMEM((n_rows, 1), F32),
                        pltpu.VMEM((n_rows, D_MODEL), F32)])
    return pl.pallas_call(
        functools.partial(_attn_sample_kernel, lam_init=lam_init, t_new=t_new),
        grid_spec=grid_spec,
        out_shape=jax.ShapeDtypeStruct((n_batch, t_new, D_MODEL), F32),
        compiler_params=_params("parallel", "arbitrary"),
        name="attn_sample",
    )(page_table, lam.reshape(1, 1), q, k_new, v_new, subw.reshape(1, D_HEAD),
      *([ck] * PAGES_PER_STEP), *([cv] * PAGES_PER_STEP))


def _router_kernel(h_ref, w_ref, b_ref, o_ref):
    h = h_ref[...]
    h_hi = h.astype(BF16)
    h_lo = (h - h_hi.astype(F32)).astype(BF16)
    w = w_ref[...]
    w_hi = w.astype(BF16)
    w_lo = (w - w_hi.astype(F32)).astype(BF16)
    o_ref[...] = (jnp.dot(h_hi, w_hi, preferred_element_type=F32) + jnp.dot(h_hi, w_lo, preferred_element_type=F32)
                  + jnp.dot(h_lo, w_hi, preferred_element_type=F32)) + b_ref[...]


def _router_logits(h, w_pad, b_pad):
    n, d = h.shape
    tm = min(512, n)
    return pl.pallas_call(
        _router_kernel,
        grid=(n // tm,),
        in_specs=[pl.BlockSpec((tm, d), lambda i: (i, 0)),
                  pl.BlockSpec(w_pad.shape, lambda i: (0, 0)),
                  pl.BlockSpec((1, V7X_LANES), lambda i: (0, 0))],
        out_specs=pl.BlockSpec((tm, V7X_LANES), lambda i: (i, 0)),
        out_shape=jax.ShapeDtypeStruct((n, V7X_LANES), F32),
        compiler_params=_params("parallel"),
        name="router_logits",
    )(h, w_pad, b_pad)


def _expert_kernel(be_ref, bv_ref, x_ref, gate_ref, wg_ref, wu_ref, wd_ref, y_ref):
    i = pl.program_id(0)

    @pl.when(bv_ref[i] > 0)
    def _():
        x = x_ref[...].astype(BF16)
        hg = jnp.dot(x, wg_ref[0], preferred_element_type=F32)
        hu = jnp.dot(x, wu_ref[0], preferred_element_type=F32)
        hdn = (hg * _sigmoid(hg)) * hu
        y_ref[...] = jnp.dot(hdn.astype(BF16), wd_ref[0], preferred_element_type=F32) * gate_ref[...]

    @pl.when(bv_ref[i] == 0)
    def _():
        y_ref[...] = jnp.zeros(y_ref.shape, F32)


def _expert_blocks(xg, gate_rows, block_expert, block_valid, wg, wu, wd):
    n_rows, d = xg.shape
    n_blocks = n_rows // MOE_ROWS
    grid_spec = pltpu.PrefetchScalarGridSpec(
        num_scalar_prefetch=2,
        grid=(n_blocks,),
        in_specs=[pl.BlockSpec((MOE_ROWS, d), lambda i, be, bv: (i, 0)),
                  pl.BlockSpec((MOE_ROWS, 1), lambda i, be, bv: (i, 0)),
                  pl.BlockSpec((1, d, D_EXPERT), lambda i, be, bv: (be[i], 0, 0)),
                  pl.BlockSpec((1, d, D_EXPERT), lambda i, be, bv: (be[i], 0, 0)),
                  pl.BlockSpec((1, D_EXPERT, d), lambda i, be, bv: (be[i], 0, 0))],
        out_specs=pl.BlockSpec((MOE_ROWS, d), lambda i, be, bv: (i, 0)))
    return pl.pallas_call(
        _expert_kernel,
        grid_spec=grid_spec,
        out_shape=jax.ShapeDtypeStruct((n_rows, d), F32),
        compiler_params=_params("arbitrary"),
        name="expert_blocks",
    )(block_expert, block_valid, xg, gate_rows, wg, wu, wd)


def _hier_moe(h, moe):
    n, d = h.shape
    logits = _router_logits(h, moe["w_router"], moe["b_router"])
    lg = logits[:, :N_GROUPS]
    le = logits[:, N_GROUPS:N_GROUPS + N_EXPERTS].reshape(n, N_GROUPS, EXPERTS_PER_GROUP)
    p_group = jax.nn.softmax(lg, axis=-1)
    grp = jnp.argmax(lg, axis=-1)
    g_gate = jnp.take_along_axis(p_group, grp[:, None], axis=-1)
    le_g = jnp.take_along_axis(le, grp[:, None, None], axis=1)[:, 0]
    top_val, top_idx = lax.top_k(le_g, TOP_K)
    gates = g_gate * jax.nn.softmax(top_val, axis=-1)
    eidx = (grp[:, None] * EXPERTS_PER_GROUP + top_idx).astype(jnp.int32)

    n_asg = n * TOP_K
    e_flat = eidx.reshape(n_asg)
    onehot = (e_flat[:, None] == jnp.arange(N_EXPERTS, dtype=jnp.int32)[None, :]).astype(jnp.int32)
    rank = jnp.take_along_axis(jnp.cumsum(onehot, axis=0), e_flat[:, None], axis=1)[:, 0] - 1
    counts = jnp.sum(onehot, axis=0)
    padded = (counts + MOE_ROWS - 1) // MOE_ROWS * MOE_ROWS
    pad_end = jnp.cumsum(padded)
    pad_start = pad_end - padded
    dest = pad_start[e_flat] + rank
    n_blocks = -(-n_asg // MOE_ROWS) + N_EXPERTS
    n_rows = n_blocks * MOE_ROWS
    block_start = jnp.arange(n_blocks, dtype=jnp.int32) * MOE_ROWS
    block_expert = jnp.minimum(jnp.searchsorted(pad_end, block_start, side="right"), N_EXPERTS - 1).astype(jnp.int32)
    block_valid = (block_start < pad_end[-1]).astype(jnp.int32)
    tok = jnp.arange(n_asg, dtype=jnp.int32) // TOP_K
    row_tok = jnp.zeros((n_rows,), jnp.int32).at[dest].set(tok)
    row_gate = jnp.zeros((n_rows, 1), F32).at[dest, 0].set(gates.reshape(n_asg))
    y_rows = _expert_blocks(h[row_tok], row_gate, block_expert, block_valid, moe["wg"], moe["wu"], moe["wd"])
    return y_rows[dest].reshape(n, TOP_K, d)


def _mixer_group(x, n_seq, t_len, pos0, shift0, wkv0, pool0, mix, norm_g, norm_ffn_g):
    zr, u = _norm_matmul(x, norm_g, mix["w_in"], (N_SHIFT, D_POOL))
    t_pad = -(-t_len // WKV_CHUNK) * WKV_CHUNK
    if t_pad != t_len:
        zr_k = jnp.pad(zr.reshape(n_seq, t_len, N_SHIFT), ((0, 0), (0, t_pad - t_len), (0, 0)))
        zr_k = zr_k.reshape(n_seq * t_pad, N_SHIFT)
    else:
        zr_k = zr
    y_rwkv, s_bd = _rwkv_mix(zr_k, n_seq, t_pad, min(t_len, WKV_CHUNK), shift0, _state_to_block_diag(wkv0), mix)
    if t_pad != t_len:
        y_rwkv = y_rwkv.reshape(n_seq, t_pad, D_RWKV)[:, :t_len].reshape(n_seq * t_len, D_RWKV)
    hist = jnp.pad(pool0, ((0, 0), (POOL_CARRY - POOL_BUF, 0), (0, 0)))
    y_pool = _pool_mix(u, n_seq, t_len, hist, pos0, mix["pool_w"], mix["pool_scale"])
    x, h = _matmul_res(x, [(y_rwkv, mix["w_out_a"]), (y_pool, mix["w_out_b"])], norm_ffn_g)
    new_shift = zr.reshape(n_seq, t_len, N_SHIFT)[:, -1]
    ext = jnp.concatenate([pool0, u.reshape(n_seq, t_len, D_POOL)], axis=1)
    return x, h, new_shift, _state_from_block_diag(s_bd), ext[:, -POOL_BUF:]


def kernel(x_prompt, x_sample, state_wkv, state_shift, state_pool, cache_k, cache_v, page_table,
           norm_mix, norm_ffn, norm_final,
           w_in_mix, mu_shift, w_decay0, w_decay_up, w_iclr0, w_iclr_up, w_gate_up,
           k_k, k_a, r_k, ln_x_w, ln_x_b, pool_w, pool_scale, w_out_mix,
           w_qkv, lam_q1, lam_k1, lam_q2, lam_k2, subln_w, w_o,
           router_group, router_group_bias, router_expert, router_expert_bias,
           w_exp_gate, w_exp_up, w_exp_down):
    bp, tp, d = x_prompt.shape
    bs, ts, _ = x_sample.shape
    depth = norm_mix.shape[0]
    past_len = page_table.shape[1] * PAGE_SIZE
    groups = [dict(x=x_prompt.reshape(bp * tp, d), n_seq=bp, t=tp), dict(x=x_sample.reshape(bs * ts, d), n_seq=bs, t=ts)]
    outs = [dict(wkv=[], shift=[], pool=[], k=[], v=[]) for _ in groups]

    for layer in range(depth):
        j = layer // 2
        if layer % 2 == 0:
            zero = jnp.zeros((DECAY_LORA, D_RWKV), F32)
            mix = dict(
                w_in=w_in_mix[j].astype(BF16), mu=mu_shift[j].reshape(1, N_SHIFT), w0=w_decay0[j].reshape(1, D_RWKV),
                wcomb=jnp.block([[w_decay_up[j], zero], [zero, w_iclr_up[j]]]).astype(BF16),
                a0=w_iclr0[j].reshape(1, D_RWKV), wgate=w_gate_up[j].astype(BF16),
                k_k=k_k[j].reshape(1, D_RWKV), k_a=k_a[j].reshape(1, D_RWKV), r_k=r_k[j].reshape(1, D_RWKV),
                ln_w=ln_x_w[j].reshape(1, D_RWKV), ln_b=ln_x_b[j].reshape(1, D_RWKV),
                pool_w=pool_w[j].astype(BF16), pool_scale=pool_scale[j],
                w_out_a=w_out_mix[j, :D_RWKV].astype(BF16), w_out_b=w_out_mix[j, D_RWKV:].astype(BF16))
            inits = [(0, jnp.zeros((bp, N_SHIFT), F32), jnp.zeros((bp, H_RWKV, HD_RWKV, HD_RWKV), F32),
                      jnp.zeros((bp, POOL_BUF, D_POOL), F32)),
                     (past_len, state_shift[j], state_wkv[j], state_pool[j])]
            for grp, out, (pos0, sh0, wk0, pl0) in zip(groups, outs, inits):
                grp["x"], grp["h"], sh, wk, po = _mixer_group(grp["x"], grp["n_seq"], grp["t"], pos0, sh0, wk0, pl0,
                                                              mix, norm_mix[layer], norm_ffn[layer])
                out["wkv"].append(wk); out["shift"].append(sh); out["pool"].append(po)
        else:
            lam_init = 0.8 - 0.6 * math.exp(-0.3 * layer)
            lam = (jnp.exp(jnp.sum(lam_q1[j] * lam_k1[j])) - jnp.exp(jnp.sum(lam_q2[j] * lam_k2[j])) + lam_init)
            wqkv = w_qkv[j].astype(BF16)
            wo = w_o[j].astype(BF16)
            for gi, (grp, out) in enumerate(zip(groups, outs)):
                n_seq, t = grp["n_seq"], grp["t"]
                q, k, v = _norm_matmul(grp["x"], norm_mix[layer], wqkv, (d, d, d))
                q3, k3, v3 = (a.reshape(n_seq, t, d) for a in (q, k, v))
                if gi == 0:
                    o = _attn_prompt(q3, k3, v3, lam, subln_w[j], lam_init, n_seq, t)
                else:
                    o = _attn_sample(q3, k3, v3, cache_k, cache_v, j, page_table, lam, subln_w[j], lam_init)
                grp["x"], grp["h"] = _matmul_res(grp["x"], [(o.reshape(n_seq * t, d), wo)], norm_ffn[layer])
                out["k"].append(k.reshape(n_seq, t, H_ATT, D_HEAD)); out["v"].append(v.reshape(n_seq, t, H_ATT, D_HEAD))

        n_router = N_GROUPS + N_EXPERTS
        moe = dict(
            w_router=jnp.pad(jnp.concatenate([router_group[layer], router_expert[layer]], axis=1),
                             ((0, 0), (0, V7X_LANES - n_router))),
            b_router=jnp.pad(jnp.concatenate([router_group_bias[layer], router_expert_bias[layer]]),
                             (0, V7X_LANES - n_router)).reshape(1, V7X_LANES),
            wg=w_exp_gate[layer].astype(BF16), wu=w_exp_up[layer].astype(BF16), wd=w_exp_down[layer].astype(BF16))
        for grp in groups:
            y2 = _hier_moe(grp["h"], moe)
            grp["x"] = _combine(grp["x"], y2, norm_final, normed=(layer == depth - 1))

    res = []
    for grp, out in zip(groups, outs):
        res.append([grp["x"].reshape(grp["n_seq"], grp["t"], d), jnp.stack(out["wkv"]), jnp.stack(out["shift"]),
                    jnp.stack(out["pool"]), jnp.stack(out["k"]), jnp.stack(out["v"])])
    (yp, wkp, shp, pop, kp, vp), (ys, wks, shs, pos_, ks_, vs_) = res
    return (yp, ys, wkp, shp, pop, kp, vp, wks, shs, pos_, ks_, vs_)
```

```python
import functools
import math

import jax
import jax.numpy as jnp
from jax import lax
from jax.experimental import pallas as pl
from jax.experimental.pallas import tpu as pltpu

F32 = jnp.float32
BF16 = jnp.bfloat16

V7X_LANES = 128
V7X_SUBLANES = 8
V7X_VMEM_LIMIT_BYTES = 56 * 1024 * 1024

D_MODEL = 1024
RMS_EPS = 1e-5
NEG_INF = -1e30
D_RWKV = 512
HD_RWKV = 64
H_RWKV = 8
N_PAIRS = H_RWKV // 2
DECAY_LORA = 64
ICLR_LORA = 64
GATE_LORA = 128
LN_X_EPS = 64e-5
D_POOL = 512
POOL_WINDOWS = (2, 4, 8, 16)
POOL_GROUP = 128
POOL_BUF = 15
POOL_CARRY = 16
N_SHIFT = 3 * D_RWKV + GATE_LORA + DECAY_LORA + ICLR_LORA
HD_ATT = 64
H_ATT = 8
D_HEAD = 2 * HD_ATT
PAGE_SIZE = 128
N_GROUPS = 4
EXPERTS_PER_GROUP = 8
N_EXPERTS = 32
TOP_K = 2
D_EXPERT = 512
WKV_CHUNK = 64
WKV_SEQS = 4
MOE_ROWS = 256
PAGES_PER_STEP = 8


def _params(*semantics):
    return pltpu.CompilerParams(dimension_semantics=semantics, vmem_limit_bytes=V7X_VMEM_LIMIT_BYTES)


_NN = (((1,), (0,)), ((), ()))
_NT = (((1,), (1,)), ((), ()))
_TN = (((0,), (0,)), ((), ()))


def _hi_lo(x):
    hi = x.astype(BF16)
    return hi, (None if x.dtype == BF16 else (x - hi.astype(F32)).astype(BF16))


def _mm(a, b, dims=_NN, precise=False):
    dot = functools.partial(lax.dot_general, dimension_numbers=dims, preferred_element_type=F32)
    if not precise:
        return dot(a.astype(BF16), b.astype(BF16))
    a_hi, a_lo = _hi_lo(a)
    b_hi, b_lo = _hi_lo(b)
    out = dot(a_hi, b_hi)
    if b_lo is not None:
        out = out + dot(a_hi, b_lo)
    if a_lo is not None:
        out = out + dot(a_lo, b_hi)
    return out


def _sigmoid(x):
    return 1.0 / (1.0 + jnp.exp(-x))


def _softplus(x):
    return jnp.maximum(x, 0.0) + jnp.log(1.0 + jnp.exp(-jnp.abs(x)))


def _rms(x, g):
    return x * lax.rsqrt(jnp.mean(x * x, axis=-1, keepdims=True) + RMS_EPS) * g


def _norm_matmul_kernel(x_ref, g_ref, w_ref, *out_refs, precise):
    h = _rms(x_ref[...], g_ref[...])
    off = 0
    for o_ref in out_refs:
        n = o_ref.shape[-1]
        o_ref[...] = _mm(h, w_ref[:, off:off + n], precise=precise)
        off += n


def _norm_matmul(x, g, w, splits, precise):
    n, d = x.shape
    tm = min(512, n)
    assert n % tm == 0 and sum(splits) == w.shape[1]
    return pl.pallas_call(
        functools.partial(_norm_matmul_kernel, precise=precise),
        grid=(n // tm,),
        in_specs=[pl.BlockSpec((tm, d), lambda i: (i, 0)),
                  pl.BlockSpec((1, d), lambda i: (0, 0)),
                  pl.BlockSpec(w.shape, lambda i: (0, 0))],
        out_specs=[pl.BlockSpec((tm, s), lambda i: (i, 0)) for s in splits],
        out_shape=[jax.ShapeDtypeStruct((n, s), F32) for s in splits],
        compiler_params=_params("parallel"),
        name="norm_matmul",
    )(x, g.reshape(1, d), w)


def _matmul_res_kernel(res_ref, g_ref, *refs, precise):
    xo_ref, ho_ref = refs[-2:]
    acc = res_ref[...]
    for a_ref, w_ref in zip(refs[0:-2:2], refs[1:-2:2]):
        acc = acc + _mm(a_ref[...], w_ref[...], precise=precise)
    xo_ref[...] = acc
    ho_ref[...] = _rms(acc, g_ref[...])


def _matmul_res(res, pairs, g, precise):
    n, d = res.shape
    tm = min(512, n)
    assert n % tm == 0
    in_specs = [pl.BlockSpec((tm, d), lambda i: (i, 0)), pl.BlockSpec((1, d), lambda i: (0, 0))]
    args = [res, g.reshape(1, d)]
    for a, w in pairs:
        in_specs += [pl.BlockSpec((tm, a.shape[1]), lambda i: (i, 0)), pl.BlockSpec(w.shape, lambda i: (0, 0))]
        args += [a, w]
    return pl.pallas_call(
        functools.partial(_matmul_res_kernel, precise=precise),
        grid=(n // tm,),
        in_specs=in_specs,
        out_specs=[pl.BlockSpec((tm, d), lambda i: (i, 0))] * 2,
        out_shape=[jax.ShapeDtypeStruct((n, d), F32)] * 2,
        compiler_params=_params("parallel"),
        name="matmul_res",
    )(*args)


def _combine_kernel(x_ref, y0_ref, y1_ref, gates_ref, g_ref, o_ref, *, normed):
    gates = gates_ref[...]
    x = x_ref[...] + (y0_ref[...] * gates[:, 0:1] + y1_ref[...] * gates[:, 1:2])
    o_ref[...] = _rms(x, g_ref[...]) if normed else x


def _combine(x, y0, y1, gates, g, normed):
    n, d = x.shape
    tm = min(512, n)
    assert n % tm == 0
    row = pl.BlockSpec((tm, d), lambda i: (i, 0))
    return pl.pallas_call(
        functools.partial(_combine_kernel, normed=normed),
        grid=(n // tm,),
        in_specs=[row, row, row, pl.BlockSpec((tm, TOP_K), lambda i: (i, 0)), pl.BlockSpec((1, d), lambda i: (0, 0))],
        out_specs=row,
        out_shape=jax.ShapeDtypeStruct((n, d), F32),
        compiler_params=_params("parallel"),
        name="combine",
    )(x, y0, y1, gates, g.reshape(1, d))


def _block_diag(x, left):
    return jnp.concatenate([jnp.where(left, x, 0.0), jnp.where(left, 0.0, x)], axis=0)


def _rwkv_kernel(zr_ref, shift0_ref, s0_ref, mu_ref, w0_ref, wcomb_ref, a0_ref, wgate_ref, kk_ref, ka_ref,
                 rk_ref, lnw_ref, lnb_ref, e_ref, y_ref, sout_ref, s_scr, prev_scr, *, t_valid, precise):
    c_len = WKV_CHUNK
    n_sb = zr_ref.shape[0]
    c = pl.program_id(1)
    mm = functools.partial(_mm, precise=precise)

    @pl.when(c == 0)
    def _():
        s_scr[...] = s0_ref[...]
        prev_scr[...] = shift0_ref[...]

    row = lax.broadcasted_iota(jnp.int32, (c_len, 1), 0)
    lane = lax.broadcasted_iota(jnp.int32, (1, V7X_LANES), 1)
    left = lane < HD_RWKV
    valid = row < t_valid
    s_le_t = (lane % HD_RWKV) <= row
    s_lt_t = (lane % HD_RWKV) < row
    same_head = (lax.broadcasted_iota(jnp.int32, (V7X_LANES, 1), 0) < HD_RWKV) == left
    n_doublings = int(math.log2(c_len))
    e = e_ref[...]

    def seg_sum(x):
        return _mm(x, e, precise=True)

    def cat(x, y):
        return jnp.concatenate([x, y], axis=0)

    def bd(x):
        return _block_diag(x, left)

    seqs = range(n_sb)
    zr = [zr_ref[sq] for sq in seqs]
    z_prev = [jnp.where(row == 0, prev_scr[sq], pltpu.roll(zr[sq], 1, 0)) for sq in seqs]
    for sq in seqs:
        prev_scr[sq] = zr[sq][c_len - 1:c_len, :]
    zs = [zr[sq] + (z_prev[sq] - zr[sq]) * mu_ref[...] for sq in seqs]
    o3 = 3 * D_RWKV
    da = [zs[sq][:, o3 + GATE_LORA:] for sq in seqs]
    pre = [mm(jnp.where(left, jnp.tanh(x), x), wcomb_ref[...]) for x in da]
    gate = [mm(_sigmoid(zs[sq][:, o3:o3 + GATE_LORA]), wgate_ref[...]) for sq in seqs]

    chains = [(sq, p) for sq in seqs for p in range(N_PAIRS)]
    idx = range(len(chains))
    sls = [slice(p * V7X_LANES, (p + 1) * V7X_LANES) for _, p in chains]

    def cols(sq, p, group):
        lo = group * D_RWKV + p * V7X_LANES
        return zs[sq][:, lo:lo + V7X_LANES]

    r = [cols(sq, p, 0) for sq, p in chains]
    k = [cols(sq, p, 1) for sq, p in chains]
    v = [cols(sq, p, 2) for sq, p in chains]
    lw = [-jnp.exp(-_softplus(-(w0_ref[:, sls[i]] + pre[sq][:, sls[i]])) - 0.5)
          for i, (sq, p) in enumerate(chains)]
    iclr = [_sigmoid(a0_ref[:, sls[i]] + pre[sq][:, D_RWKV + p * V7X_LANES:D_RWKV + (p + 1) * V7X_LANES])
            for i, (sq, p) in enumerate(chains)]
    kk = [k[i] * kk_ref[:, sls[i]] for i in idx]
    kk_ss = [seg_sum(kk[i] * kk[i]) for i in idx]
    kk = [kk[i] * lax.rsqrt(jnp.maximum(kk_ss[i], 1e-24)) for i in idx]
    k = [k[i] * (1.0 + (iclr[i] - 1.0) * ka_ref[:, sls[i]]) for i in idx]
    a = [-kk[i] for i in idx]
    b = [kk[i] * iclr[i] for i in idx]
    if t_valid < c_len:
        lw, a, b, k, v = ([jnp.where(valid, x, 0.0) for x in xs] for xs in (lw, a, b, k, v))

    def cumsum_rows(x):
        sh = 1
        while sh < c_len:
            x = x + jnp.where(row >= sh, pltpu.roll(x, sh, 0), 0.0)
            sh *= 2
        return x

    g = [cumsum_rows(x) for x in lw]
    e_g = [jnp.exp(x) for x in g]
    e_ng = [jnp.exp(-x) for x in g]
    g_end = [x[c_len - 1:c_len, :] for x in e_g]
    a_t = [a[i] * jnp.exp(g[i] - lw[i]) for i in idx]
    b_t = [b[i] * e_ng[i] for i in idx]
    k_t = [k[i] * e_ng[i] for i in idx]
    r_t = [r[i] * e_g[i] for i in idx]

    s = [s_scr[sq, p] for sq, p in chains]
    bd_v = [bd(x) for x in v]
    sc = [mm(cat(a_t[i], r_t[i]), cat(bd(b_t[i]), bd(k_t[i])), _NT) for i in idx]
    a_ab = [jnp.where(s_lt_t, x[:c_len, :V7X_LANES], 0.0) for x in sc]
    a_ak = [jnp.where(s_lt_t, x[:c_len, V7X_LANES:], 0.0) for x in sc]
    a_rb = [jnp.where(s_le_t, x[c_len:, :V7X_LANES], 0.0) for x in sc]
    a_rk = [jnp.where(s_le_t, x[c_len:, V7X_LANES:], 0.0) for x in sc]

    x = [mm(a_t[i], s[i], _NT) + mm(a_ak[i], bd_v[i]) for i in idx]
    lmat = a_ab
    for it in range(n_doublings):
        x = [x[i] + mm(lmat[i], bd(x[i])) for i in idx]
        if it + 1 < n_doublings:
            lmat = [mm(lmat[i], bd(lmat[i])) for i in idx]
    u = x
    y = [mm(r_t[i], s[i], _NT) + mm(a_rb[i], bd(u[i])) + mm(a_rk[i], bd_v[i]) for i in idx]
    s_new = [s[i] * g_end[i] + mm(cat(u[i], v[i]), cat(b_t[i] * g_end[i], k_t[i] * g_end[i]), _TN) for i in idx]
    for i, (sq, p) in enumerate(chains):
        s_scr[sq, p] = jnp.where(same_head, s_new[i], 0.0)

    mean = [seg_sum(x) * (1.0 / HD_RWKV) for x in y]
    d = [y[i] - mean[i] for i in idx]
    var = [seg_sum(x * x) * (1.0 / HD_RWKV) for x in d]
    bonus_rk = [seg_sum(r[i] * k[i] * rk_ref[:, sls[i]]) for i in idx]
    for i, (sq, p) in enumerate(chains):
        yn = d[i] * lax.rsqrt(var[i] + LN_X_EPS) * lnw_ref[:, sls[i]] + lnb_ref[:, sls[i]]
        y_ref[sq, :, sls[i]] = (yn + bonus_rk[i] * v[i]) * gate[sq][:, sls[i]]

    sout_ref[...] = s_scr[...]


def _rwkv_mix(zr, n_seq, t_pad, t_valid, shift0, s0_bd, mix, precise):
    n_chunks = t_pad // WKV_CHUNK
    n_sb = math.gcd(n_seq, WKV_SEQS)
    assert t_pad % WKV_CHUNK == 0 and (t_valid == WKV_CHUNK or n_chunks == 1)
    rowvec = lambda n: pl.BlockSpec((1, n), lambda s, c: (0, 0))
    full = lambda a: pl.BlockSpec(a.shape, lambda s, c: (0,) * a.ndim)
    state_spec = pl.BlockSpec((n_sb, N_PAIRS, V7X_LANES, V7X_LANES), lambda s, c: (s, 0, 0, 0))
    e128 = jnp.kron(jnp.eye(2, dtype=F32), jnp.ones((HD_RWKV, HD_RWKV), F32)).astype(BF16)
    return pl.pallas_call(
        functools.partial(_rwkv_kernel, t_valid=t_valid, precise=precise),
        grid=(n_seq // n_sb, n_chunks),
        in_specs=[pl.BlockSpec((n_sb, WKV_CHUNK, N_SHIFT), lambda s, c: (s, c, 0)),
                  pl.BlockSpec((n_sb, 1, N_SHIFT), lambda s, c: (s, 0, 0)),
                  state_spec,
                  rowvec(N_SHIFT), rowvec(D_RWKV), full(mix["wcomb"]), rowvec(D_RWKV), full(mix["wgate"]),
                  rowvec(D_RWKV), rowvec(D_RWKV), rowvec(D_RWKV), rowvec(D_RWKV), rowvec(D_RWKV), full(e128)],
        out_specs=[pl.BlockSpec((n_sb, WKV_CHUNK, D_RWKV), lambda s, c: (s, c, 0)), state_spec],
        out_shape=[jax.ShapeDtypeStruct((n_seq, t_pad, D_RWKV), F32),
                   jax.ShapeDtypeStruct((n_seq, N_PAIRS, V7X_LANES, V7X_LANES), F32)],
        scratch_shapes=[pltpu.VMEM((n_sb, N_PAIRS, V7X_LANES, V7X_LANES), F32),
                        pltpu.VMEM((n_sb, 1, N_SHIFT), F32)],
        compiler_params=_params("parallel", "arbitrary"),
        name="rwkv_mix",
    )(zr, shift0.reshape(n_seq, 1, N_SHIFT), s0_bd, mix["mu"], mix["w0"], mix["wcomb"], mix["a0"], mix["wgate"],
      mix["k_k"], mix["k_a"], mix["r_k"], mix["ln_w"], mix["ln_b"], e128)


def _state_to_block_diag(s):
    n = s.shape[0]
    s = s.reshape(n, N_PAIRS, 2, HD_RWKV, HD_RWKV)
    eye = jnp.eye(2, dtype=s.dtype)
    bd = jnp.einsum("nphvk,hg->nphvgk", s, eye)
    return bd.reshape(n, N_PAIRS, V7X_LANES, V7X_LANES)


def _state_from_block_diag(bd):
    n = bd.shape[0]
    bd = bd.reshape(n, N_PAIRS, 2, HD_RWKV, 2, HD_RWKV)
    s = jnp.stack([bd[:, :, 0, :, 0, :], bd[:, :, 1, :, 1, :]], axis=2)
    return s.reshape(n, H_RWKV, HD_RWKV, HD_RWKV)


def _pool_kernel(u_ref, hist_ref, w_ref, scale_ref, y_ref, carry_scr, *, pos0, precise):
    t = pl.program_id(1)
    tm = u_ref.shape[0]

    @pl.when(t == 0)
    def _():
        carry_scr[...] = hist_ref[0]

    u = u_ref[...]
    ext = jnp.concatenate([carry_scr[...], u], axis=0)
    carry_scr[...] = ext[tm:, :]
    pos = pos0 + t * tm + lax.broadcasted_iota(jnp.int32, (tm, 1), 0)
    acc = ext
    sums = []
    for win in POOL_WINDOWS:
        acc = acc + pltpu.roll(acc, win // 2, 0)
        sums.append(acc[POOL_CARRY:, :])
    for gi, win in enumerate(POOL_WINDOWS):
        sl = slice(gi * POOL_GROUP, (gi + 1) * POOL_GROUP)
        count = jnp.minimum(pos + 1, win).astype(F32)
        x = sums[gi][:, sl] / count - u[:, sl]
        y_ref[:, sl] = _mm(x, w_ref[gi], precise=precise) * scale_ref[:, sl]


def _pool_mix(u, n_seq, t_len, hist, pos0, pool_w_bf16, pool_scale, precise):
    tm = min(512, t_len)
    assert t_len % tm == 0
    n_t = t_len // tm
    return pl.pallas_call(
        functools.partial(_pool_kernel, pos0=pos0, precise=precise),
        grid=(n_seq, n_t),
        in_specs=[pl.BlockSpec((tm, D_POOL), lambda s, t: (s * n_t + t, 0)),
                  pl.BlockSpec((1, POOL_CARRY, D_POOL), lambda s, t: (s, 0, 0)),
                  pl.BlockSpec(pool_w_bf16.shape, lambda s, t: (0, 0, 0)),
                  pl.BlockSpec((1, D_POOL), lambda s, t: (0, 0))],
        out_specs=pl.BlockSpec((tm, D_POOL), lambda s, t: (s * n_t + t, 0)),
        out_shape=jax.ShapeDtypeStruct((n_seq * t_len, D_POOL), F32),
        scratch_shapes=[pltpu.VMEM((POOL_CARRY, D_POOL), F32)],
        compiler_params=_params("parallel", "arbitrary"),
        name="pool_mix",
    )(u, hist, pool_w_bf16, pool_scale.reshape(1, D_POOL))


def _sub_norm(acc, l, lam, subw, lam_init):
    tq = acc.shape[0] // 2
    o = acc[:tq] / l[:tq] - lam * (acc[tq:] / l[tq:])
    return o * lax.rsqrt(jnp.mean(o * o, axis=-1, keepdims=True) + RMS_EPS) * subw * (1.0 - lam_init)


def _attn_prompt_kernel(lam_ref, q_ref, k_ref, v_ref, subw_ref, o_ref, kb_scr, va_scr, m_scr, acc_scr, *,
                        lam_init, strip):
    tq = q_ref.shape[1]
    qi = pl.program_id(2)
    lane = lax.broadcasted_iota(jnp.int32, (1, D_HEAD), 1)

    @pl.when(qi == 0)
    def _():
        kb_scr[...] = k_ref[0].astype(BF16)
        va_scr[:, :D_HEAD] = v_ref[0].astype(BF16)
        va_scr[:, D_HEAD:] = jnp.ones((va_scr.shape[0], D_HEAD), BF16)

    q = q_ref[0] * (HD_ATT ** -0.5)
    qs = jnp.concatenate([jnp.where(lane < HD_ATT, q, 0.0), jnp.where(lane < HD_ATT, 0.0, q)], axis=0).astype(BF16)
    m_scr[...] = jnp.full(m_scr.shape, NEG_INF, F32)
    acc_scr[...] = jnp.zeros(acc_scr.shape, F32)
    n_lane_blocks = tq // V7X_LANES

    strips = [slice(r, r + strip) for r in range(0, 2 * tq, strip)]

    def process(blocks):
        starts = [pl.multiple_of(j * tq, tq) for j, _ in blocks]
        scores = [[lax.dot_general(qs[rows], kb_scr[pl.ds(st, tq), :], _NT, preferred_element_type=F32)
                   for rows in strips] for st in starts]
        for (_, masked), st, s_blk in zip(blocks, starts, scores):
            va = va_scr[pl.ds(st, tq), :]
            for rows, s in zip(strips, s_blk):
                if masked:
                    qpos = (rows.start % tq) + lax.broadcasted_iota(jnp.int32, (strip, 1), 0)
                    kpos = lax.broadcasted_iota(jnp.int32, (1, tq), 1)
                    s = jnp.where(kpos <= qpos, s, NEG_INF)
                m_old = m_scr[rows, :]
                m_new = jnp.maximum(m_old, jnp.max(s, axis=-1, keepdims=True))
                alpha = jnp.exp(m_old - m_new)
                p = jnp.concatenate([jnp.exp(s[:, c * V7X_LANES:(c + 1) * V7X_LANES] - m_new)
                                     for c in range(n_lane_blocks)], axis=1).astype(BF16)
                pv = jnp.dot(p, va, preferred_element_type=F32)
                acc_scr[rows, :] = jnp.concatenate([alpha, alpha], axis=1) * acc_scr[rows, :] + pv
                m_scr[rows, :] = m_new

    def body(jj, carry):
        process([(2 * jj, False), (2 * jj + 1, False)])
        return carry

    lax.fori_loop(0, qi // 2, body, 0)

    @pl.when(qi % 2 == 1)
    def _():
        process([(qi - 1, False), (qi, True)])

    @pl.when(qi % 2 == 0)
    def _():
        process([(qi, True)])

    acc = acc_scr[...]
    o_ref[0] = _sub_norm(acc[:, :D_HEAD], acc[:, D_HEAD:], lam_ref[0, 0], subw_ref[...], lam_init)


def _attn_prompt(q, k, v, lam, subw, lam_init, n_batch, t_len):
    tq = min(512, t_len)
    strip = min(128, tq)
    assert t_len % tq == 0
    kv_spec = pl.BlockSpec((1, t_len, D_HEAD), lambda b, h, i: (b, 0, h))
    return pl.pallas_call(
        functools.partial(_attn_prompt_kernel, lam_init=lam_init, strip=strip),
        grid=(n_batch, H_ATT, t_len // tq),
        in_specs=[pl.BlockSpec(memory_space=pltpu.SMEM),
                  pl.BlockSpec((1, tq, D_HEAD), lambda b, h, i: (b, i, h)),
                  kv_spec, kv_spec,
                  pl.BlockSpec((1, D_HEAD), lambda b, h, i: (0, 0))],
        out_specs=pl.BlockSpec((1, tq, D_HEAD), lambda b, h, i: (b, i, h)),
        out_shape=jax.ShapeDtypeStruct((n_batch, t_len, D_MODEL), F32),
        scratch_shapes=[pltpu.VMEM((t_len, D_HEAD), BF16), pltpu.VMEM((t_len, 2 * D_HEAD), BF16),
                        pltpu.VMEM((2 * tq, V7X_LANES), F32), pltpu.VMEM((2 * tq, 2 * D_HEAD), F32)],
        compiler_params=_params("parallel", "parallel", "arbitrary"),
        name="attn_prompt",
    )(lam.reshape(1, 1), q, k, v, subw.reshape(1, D_HEAD))


def _attn_sample_kernel(pt_ref, lam_ref, q_ref, kn_ref, vn_ref, subw_ref, *refs, lam_init, t_new):
    k_refs = refs[:PAGES_PER_STEP]
    v_refs = refs[PAGES_PER_STEP:2 * PAGES_PER_STEP]
    o_ref, qm_scr, bias_scr, m_scr, l_scr, acc_scr = refs[2 * PAGES_PER_STEP:]
    g = pl.program_id(1)
    n_rows = 2 * H_ATT * t_new
    n_cols = PAGE_SIZE * H_ATT
    row_head = lax.broadcasted_iota(jnp.int32, (n_rows, 1), 0) // (2 * t_new)
    lane = lax.broadcasted_iota(jnp.int32, (1, D_HEAD), 1)

    @pl.when(g == 0)
    def _():
        q = q_ref[0] * (HD_ATT ** -0.5)
        pieces = []
        for h in range(H_ATT):
            qh = q[:, h * D_HEAD:(h + 1) * D_HEAD]
            pieces += [jnp.where(lane < HD_ATT, qh, 0.0), jnp.where(lane < HD_ATT, 0.0, qh)]
        qm_scr[...] = jnp.concatenate(pieces, axis=0).astype(BF16)
        col_head = lax.broadcasted_iota(jnp.int32, (1, n_cols), 1) % H_ATT
        bias_scr[...] = jnp.where(row_head == col_head, 0.0, NEG_INF)
        m_scr[...] = jnp.full(m_scr.shape, NEG_INF, F32)
        l_scr[...] = jnp.zeros(l_scr.shape, F32)
        acc_scr[...] = jnp.zeros(acc_scr.shape, F32)

    def update(scores, values):
        m_old = m_scr[...]
        m_new = m_old
        for s in scores:
            m_new = jnp.maximum(m_new, jnp.max(s, axis=-1, keepdims=True))
        alpha = jnp.exp(m_old - m_new)
        l_new = alpha * l_scr[...]
        acc = alpha * acc_scr[...]
        for s, vmat in zip(scores, values):
            p = jnp.concatenate([jnp.exp(s[:, c * V7X_LANES:(c + 1) * V7X_LANES] - m_new)
                                 for c in range(s.shape[1] // V7X_LANES)], axis=1)
            l_new = l_new + jnp.sum(p, axis=-1, keepdims=True)
            acc = acc + jnp.dot(p.astype(BF16), vmat, preferred_element_type=F32)
        m_scr[...] = m_new
        l_scr[...] = l_new
        acc_scr[...] = acc

    qm = qm_scr[...]
    scores, values = [], []
    for i in range(PAGES_PER_STEP):
        kf = k_refs[i][0, 0].reshape(n_cols, D_HEAD).astype(BF16)
        values.append(v_refs[i][0, 0].reshape(n_cols, D_HEAD).astype(BF16))
        scores.append(lax.dot_general(qm, kf, (((1,), (1,)), ((), ())), preferred_element_type=F32) + bias_scr[...])
    update(scores, values)

    @pl.when(g == pl.num_programs(1) - 1)
    def _():
        n_new = t_new * H_ATT
        pad = jnp.zeros((V7X_LANES - n_new, D_HEAD), BF16)
        kn = jnp.concatenate([kn_ref[0].reshape(n_new, D_HEAD).astype(BF16), pad], axis=0)
        vn = jnp.concatenate([vn_ref[0].reshape(n_new, D_HEAD).astype(BF16), pad], axis=0)
        s_new = lax.dot_general(qm, kn, (((1,), (1,)), ((), ())), preferred_element_type=F32)
        col = lax.broadcasted_iota(jnp.int32, (1, V7X_LANES), 1)
        qpos = lax.broadcasted_iota(jnp.int32, (n_rows, 1), 0) % t_new
        ok = (col < n_new) & (col % H_ATT == row_head) & (col // H_ATT <= qpos)
        update([jnp.where(ok, s_new, NEG_INF)], [vn])
        acc = acc_scr[...]
        l = l_scr[...]
        lam = lam_ref[0, 0]
        for h in range(H_ATT):
            rows = slice(2 * h * t_new, 2 * (h + 1) * t_new)
            o_ref[0, :, h * D_HEAD:(h + 1) * D_HEAD] = _sub_norm(acc[rows], l[rows], lam, subw_ref[...], lam_init)


def _attn_sample(q, k_new, v_new, cache_k, cache_v, layer_j, page_table, lam, subw, lam_init):
    n_batch, t_new, _ = q.shape
    n_pages = page_table.shape[1]
    assert n_pages % PAGES_PER_STEP == 0 and t_new <= PAGE_SIZE and t_new % V7X_SUBLANES == 0
    n_steps = n_pages // PAGES_PER_STEP
    n_rows = 2 * H_ATT * t_new
    assert n_rows == V7X_LANES and t_new * H_ATT <= V7X_LANES

    def page_spec(i):
        return pl.BlockSpec((1, 1, PAGE_SIZE, H_ATT, D_HEAD),
                            lambda b, g, pt: (layer_j, pt[b, g * PAGES_PER_STEP + i], 0, 0, 0))

    tok_spec = pl.BlockSpec((1, t_new, D_MODEL), lambda b, g, pt: (b, 0, 0))
    new_spec = pl.BlockSpec((1, t_new, H_ATT, D_HEAD), lambda b, g, pt: (b, 0, 0, 0))
    grid_spec = pltpu.PrefetchScalarGridSpec(
        num_scalar_prefetch=1,
        grid=(n_batch, n_steps),
        in_specs=[pl.BlockSpec(memory_space=pltpu.SMEM), tok_spec, new_spec, new_spec,
                  pl.BlockSpec((1, D_HEAD), lambda b, g, pt: (0, 0))]
                 + [page_spec(i) for i in range(PAGES_PER_STEP)] * 2,
        out_specs=tok_spec,
        scratch_shapes=[pltpu.VMEM((n_rows, D_HEAD), BF16),
                        pltpu.VMEM((n_rows, PAGE_SIZE * H_ATT), F32),
                        pltpu.VMEM((n_rows, V7X_LANES), F32), pltpu.VMEM((n_rows, V7X_LANES), F32),
                        pltpu.VMEM((n_rows, D_HEAD), F32)])
    new_shape = (n_batch, t_new, H_ATT, D_HEAD)
    return pl.pallas_call(
        functools.partial(_attn_sample_kernel, lam_init=lam_init, t_new=t_new),
        grid_spec=grid_spec,
        out_shape=jax.ShapeDtypeStruct((n_batch, t_new, D_MODEL), F32),
        compiler_params=_params("parallel", "arbitrary"),
        name="attn_sample",
    )(page_table, lam.reshape(1, 1), q, k_new.reshape(new_shape), v_new.reshape(new_shape), subw.reshape(1, D_HEAD),
      *([cache_k] * PAGES_PER_STEP), *([cache_v] * PAGES_PER_STEP))


def _router_kernel(h_ref, w_ref, b_ref, o_ref):
    o_ref[...] = _mm(h_ref[...], w_ref[...], precise=True) + b_ref[...]


def _router_logits(h, w_pad, b_pad):
    n, d = h.shape
    tm = min(512, n)
    return pl.pallas_call(
        _router_kernel,
        grid=(n // tm,),
        in_specs=[pl.BlockSpec((tm, d), lambda i: (i, 0)),
                  pl.BlockSpec(w_pad.shape, lambda i: (0, 0)),
                  pl.BlockSpec((1, V7X_LANES), lambda i: (0, 0))],
        out_specs=pl.BlockSpec((tm, V7X_LANES), lambda i: (i, 0)),
        out_shape=jax.ShapeDtypeStruct((n, V7X_LANES), F32),
        compiler_params=_params("parallel"),
        name="router_logits",
    )(h, w_pad, b_pad)


def _expert_kernel(be_ref, bv_ref, x_ref, wg_ref, wu_ref, wd_ref, y_ref, wg_scr, wu_scr, wd_scr):
    i = pl.program_id(0)

    @pl.when((i == 0) | (be_ref[i] != be_ref[jnp.maximum(i - 1, 0)]))
    def _():
        wg_scr[...] = wg_ref[0, 0].astype(BF16)
        wu_scr[...] = wu_ref[0, 0].astype(BF16)
        wd_scr[...] = wd_ref[0, 0].astype(BF16)

    @pl.when(bv_ref[i] > 0)
    def _():
        x = x_ref[...].astype(BF16)
        hg = jnp.dot(x, wg_scr[...], preferred_element_type=F32)
        hu = jnp.dot(x, wu_scr[...], preferred_element_type=F32)
        hdn = (hg * _sigmoid(hg)) * hu
        y_ref[...] = jnp.dot(hdn.astype(BF16), wd_scr[...], preferred_element_type=F32)

    @pl.when(bv_ref[i] == 0)
    def _():
        y_ref[...] = jnp.zeros(y_ref.shape, F32)


def _expert_blocks(xg, block_expert, block_valid, wg, wu, wd, layer):
    n_rows, d = xg.shape
    n_blocks = n_rows // MOE_ROWS
    grid_spec = pltpu.PrefetchScalarGridSpec(
        num_scalar_prefetch=2,
        grid=(n_blocks,),
        in_specs=[pl.BlockSpec((MOE_ROWS, d), lambda i, be, bv: (i, 0)),
                  pl.BlockSpec((1, 1, d, D_EXPERT), lambda i, be, bv: (layer, be[i], 0, 0)),
                  pl.BlockSpec((1, 1, d, D_EXPERT), lambda i, be, bv: (layer, be[i], 0, 0)),
                  pl.BlockSpec((1, 1, D_EXPERT, d), lambda i, be, bv: (layer, be[i], 0, 0))],
        out_specs=pl.BlockSpec((MOE_ROWS, d), lambda i, be, bv: (i, 0)),
        scratch_shapes=[pltpu.VMEM((d, D_EXPERT), BF16), pltpu.VMEM((d, D_EXPERT), BF16),
                        pltpu.VMEM((D_EXPERT, d), BF16)])
    return pl.pallas_call(
        _expert_kernel,
        grid_spec=grid_spec,
        out_shape=jax.ShapeDtypeStruct((n_rows, d), F32),
        compiler_params=_params("arbitrary"),
        name="expert_blocks",
    )(block_expert, block_valid, xg, wg, wu, wd)


def _hier_moe(h, moe):
    n, d = h.shape
    logits = _router_logits(h, moe["w_router"], moe["b_router"])
    lg = logits[:, :N_GROUPS]
    le = logits[:, N_GROUPS:N_GROUPS + N_EXPERTS].reshape(n, N_GROUPS, EXPERTS_PER_GROUP)
    p_group = jax.nn.softmax(lg, axis=-1)
    grp = jnp.argmax(lg, axis=-1)
    g_gate = jnp.take_along_axis(p_group, grp[:, None], axis=-1)
    le_g = jnp.take_along_axis(le, grp[:, None, None], axis=1)[:, 0]
    top_val, top_idx = lax.top_k(le_g, TOP_K)
    gates = g_gate * jax.nn.softmax(top_val, axis=-1)
    eidx = (grp[:, None] * EXPERTS_PER_GROUP + top_idx).astype(jnp.int32)

    n_asg = n * TOP_K
    e_flat = eidx.reshape(n_asg)
    onehot = (e_flat[:, None] == jnp.arange(N_EXPERTS, dtype=jnp.int32)[None, :]).astype(jnp.int32)
    rank = jnp.take_along_axis(jnp.cumsum(onehot, axis=0), e_flat[:, None], axis=1)[:, 0] - 1
    counts = jnp.sum(onehot, axis=0)
    padded = (counts + MOE_ROWS - 1) // MOE_ROWS * MOE_ROWS
    pad_end = jnp.cumsum(padded)
    pad_start = pad_end - padded
    dest = pad_start[e_flat] + rank
    n_blocks = -(-n_asg // MOE_ROWS) + N_EXPERTS
    n_rows = n_blocks * MOE_ROWS
    block_start = jnp.arange(n_blocks, dtype=jnp.int32) * MOE_ROWS
    block_expert = jnp.sum((block_start[:, None] >= pad_end[None, :]).astype(jnp.int32), axis=1)
    block_expert = jnp.minimum(block_expert, N_EXPERTS - 1)
    block_valid = (block_start < pad_end[-1]).astype(jnp.int32)
    tok = jnp.arange(n_asg, dtype=jnp.int32) // TOP_K
    row_tok = jnp.zeros((n_rows,), jnp.int32).at[dest].set(tok)
    y_rows = _expert_blocks(h[row_tok], block_expert, block_valid, moe["wg"], moe["wu"], moe["wd"], moe["layer"])
    dest = dest.reshape(n, TOP_K)
    return y_rows[dest[:, 0]], y_rows[dest[:, 1]], gates


def _mix_params(j, precise, w_in_mix, mu_shift, w_decay0, w_decay_up, w_iclr0, w_iclr_up, w_gate_up,
                k_k, k_a, r_k, ln_x_w, ln_x_b, pool_w, pool_scale, w_out_mix):
    wdt = F32 if precise else BF16
    zero = jnp.zeros((DECAY_LORA, D_RWKV), F32)
    vec = lambda a: a[j].reshape(1, -1)
    return dict(
        w_in=w_in_mix[j].astype(wdt), mu=vec(mu_shift), w0=vec(w_decay0),
        wcomb=jnp.block([[w_decay_up[j], zero], [zero, w_iclr_up[j]]]).astype(wdt),
        a0=vec(w_iclr0), wgate=w_gate_up[j].astype(wdt), k_k=vec(k_k), k_a=vec(k_a), r_k=vec(r_k),
        ln_w=vec(ln_x_w), ln_b=vec(ln_x_b), pool_w=pool_w[j].astype(wdt), pool_scale=pool_scale[j],
        w_out_a=w_out_mix[j, :D_RWKV].astype(wdt), w_out_b=w_out_mix[j, D_RWKV:].astype(wdt))


def _mixer_group(x, n_seq, t_len, pos0, shift0, wkv0, pool0, mix, norm_g, norm_ffn_g, precise):
    zr, u = _norm_matmul(x, norm_g, mix["w_in"], (N_SHIFT, D_POOL), precise)
    zr = zr.reshape(n_seq, t_len, N_SHIFT)
    t_pad = -(-t_len // WKV_CHUNK) * WKV_CHUNK
    zr_pad = jnp.pad(zr, ((0, 0), (0, t_pad - t_len), (0, 0))) if t_pad != t_len else zr
    y_rwkv, s_bd = _rwkv_mix(zr_pad, n_seq, t_pad, min(t_len, WKV_CHUNK), shift0, _state_to_block_diag(wkv0),
                             mix, precise)
    y_rwkv = y_rwkv[:, :t_len].reshape(n_seq * t_len, D_RWKV)
    hist = jnp.pad(pool0, ((0, 0), (POOL_CARRY - POOL_BUF, 0), (0, 0)))
    y_pool = _pool_mix(u, n_seq, t_len, hist, pos0, mix["pool_w"], mix["pool_scale"], precise)
    x, h = _matmul_res(x, [(y_rwkv, mix["w_out_a"]), (y_pool, mix["w_out_b"])], norm_ffn_g, precise)
    ext = jnp.concatenate([pool0, u.reshape(n_seq, t_len, D_POOL)], axis=1)
    return x, h, zr[:, -1], _state_from_block_diag(s_bd), ext[:, -POOL_BUF:]


def kernel(x_prompt, x_sample, state_wkv, state_shift, state_pool, cache_k, cache_v, page_table,
           norm_mix, norm_ffn, norm_final,
           w_in_mix, mu_shift, w_decay0, w_decay_up, w_iclr0, w_iclr_up, w_gate_up,
           k_k, k_a, r_k, ln_x_w, ln_x_b, pool_w, pool_scale, w_out_mix,
           w_qkv, lam_q1, lam_k1, lam_q2, lam_k2, subln_w, w_o,
           router_group, router_group_bias, router_expert, router_expert_bias,
           w_exp_gate, w_exp_up, w_exp_down):
    bp, tp, d = x_prompt.shape
    bs, ts, _ = x_sample.shape
    depth = norm_mix.shape[0]
    past_len = page_table.shape[1] * PAGE_SIZE
    groups = [dict(x=x_prompt.reshape(bp * tp, d), n_seq=bp, t=tp, precise=False),
              dict(x=x_sample.reshape(bs * ts, d), n_seq=bs, t=ts, precise=True)]
    outs = [dict(wkv=[], shift=[], pool=[], k=[], v=[]) for _ in groups]

    for layer in range(depth):
        j = layer // 2
        if layer % 2 == 0:
            mix_w = (w_in_mix, mu_shift, w_decay0, w_decay_up, w_iclr0, w_iclr_up, w_gate_up,
                     k_k, k_a, r_k, ln_x_w, ln_x_b, pool_w, pool_scale, w_out_mix)
            inits = [(0, jnp.zeros((bp, N_SHIFT), F32), jnp.zeros((bp, H_RWKV, HD_RWKV, HD_RWKV), F32),
                      jnp.zeros((bp, POOL_BUF, D_POOL), F32)),
                     (past_len, state_shift[j], state_wkv[j], state_pool[j])]
            for grp, out, (pos0, sh0, wk0, pl0) in zip(groups, outs, inits):
                mix = _mix_params(j, grp["precise"], *mix_w)
                grp["x"], grp["h"], sh, wk, po = _mixer_group(grp["x"], grp["n_seq"], grp["t"], pos0, sh0, wk0, pl0,
                                                              mix, norm_mix[layer], norm_ffn[layer], grp["precise"])
                out["wkv"].append(wk); out["shift"].append(sh); out["pool"].append(po)
        else:
            lam_init = 0.8 - 0.6 * math.exp(-0.3 * layer)
            lam = (jnp.exp(jnp.sum(lam_q1[j] * lam_k1[j])) - jnp.exp(jnp.sum(lam_q2[j] * lam_k2[j])) + lam_init)
            for gi, (grp, out) in enumerate(zip(groups, outs)):
                n_seq, t, precise = grp["n_seq"], grp["t"], grp["precise"]
                wdt = F32 if precise else BF16
                q, k, v = _norm_matmul(grp["x"], norm_mix[layer], w_qkv[j].astype(wdt), (d, d, d), precise)
                q3, k3, v3 = (a.reshape(n_seq, t, d) for a in (q, k, v))
                if gi == 0:
                    o = _attn_prompt(q3, k3, v3, lam, subln_w[j], lam_init, n_seq, t)
                else:
                    o = _attn_sample(q3, k3, v3, cache_k, cache_v, j, page_table, lam, subln_w[j], lam_init)
                grp["x"], grp["h"] = _matmul_res(grp["x"], [(o.reshape(n_seq * t, d), w_o[j].astype(wdt))],
                                                 norm_ffn[layer], precise)
                out["k"].append(k.reshape(n_seq, t, H_ATT, D_HEAD)); out["v"].append(v.reshape(n_seq, t, H_ATT, D_HEAD))

        n_router = N_GROUPS + N_EXPERTS
        moe = dict(
            w_router=jnp.pad(jnp.concatenate([router_group[layer], router_expert[layer]], axis=1),
                             ((0, 0), (0, V7X_LANES - n_router))),
            b_router=jnp.pad(jnp.concatenate([router_group_bias[layer], router_expert_bias[layer]]),
                             (0, V7X_LANES - n_router)).reshape(1, V7X_LANES),
            wg=w_exp_gate, wu=w_exp_up, wd=w_exp_down, layer=layer)
        for grp in groups:
            y0, y1, gates = _hier_moe(grp["h"], moe)
            grp["x"] = _combine(grp["x"], y0, y1, gates, norm_final, normed=(layer == depth - 1))

    res = []
    for grp, out in zip(groups, outs):
        res.append([grp["x"].reshape(grp["n_seq"], grp["t"], d), jnp.stack(out["wkv"]), jnp.stack(out["shift"]),
                    jnp.stack(out["pool"]), jnp.stack(out["k"]), jnp.stack(out["v"])])
    (yp, wkp, shp, pop, kp, vp), (ys, wks, shs, pos_, ks_, vs_) = res
    return (yp, ys, wkp, shp, pop, kp, vp, wks, shs, pos_, ks_, vs_)
```

```python
import functools
import math

import jax
import jax.numpy as jnp
from jax import lax
from jax.experimental import pallas as pl
from jax.experimental.pallas import tpu as pltpu

F32 = jnp.float32
BF16 = jnp.bfloat16

V7X_LANES = 128
V7X_SUBLANES = 8
V7X_VMEM_LIMIT_BYTES = 56 * 1024 * 1024

D_MODEL = 1024
RMS_EPS = 1e-5
NEG_INF = -1e30
D_RWKV = 512
HD_RWKV = 64
H_RWKV = 8
N_PAIRS = H_RWKV // 2
DECAY_LORA = 64
ICLR_LORA = 64
GATE_LORA = 128
LN_X_EPS = 64e-5
D_POOL = 512
POOL_WINDOWS = (2, 4, 8, 16)
POOL_GROUP = 128
POOL_BUF = 15
POOL_CARRY = 16
N_SHIFT = 3 * D_RWKV + GATE_LORA + DECAY_LORA + ICLR_LORA
HD_ATT = 64
H_ATT = 8
D_HEAD = 2 * HD_ATT
PAGE_SIZE = 128
N_GROUPS = 4
EXPERTS_PER_GROUP = 8
N_EXPERTS = 32
TOP_K = 2
D_EXPERT = 512
WKV_CHUNK = 64
WKV_SEQS = 4
MOE_ROWS = 256
PAGES_PER_STEP = 16
ATTN_STRIP = 128
ATTN_LOOKAHEAD = 16


def _params(*semantics):
    return pltpu.CompilerParams(dimension_semantics=semantics, vmem_limit_bytes=V7X_VMEM_LIMIT_BYTES)


_NN = (((1,), (0,)), ((), ()))
_NT = (((1,), (1,)), ((), ()))
_TN = (((0,), (0,)), ((), ()))


def _hi_lo(x):
    hi = x.astype(BF16)
    return hi, (None if x.dtype == BF16 else (x - hi.astype(F32)).astype(BF16))


def _mm(a, b, dims=_NN, precise=False):
    dot = functools.partial(lax.dot_general, dimension_numbers=dims, preferred_element_type=F32)
    if not precise:
        return dot(a.astype(BF16), b.astype(BF16))
    a_hi, a_lo = _hi_lo(a)
    b_hi, b_lo = _hi_lo(b)
    out = dot(a_hi, b_hi)
    if b_lo is not None:
        out = out + dot(a_hi, b_lo)
    if a_lo is not None:
        out = out + dot(a_lo, b_hi)
    return out


def _sigmoid(x):
    return 1.0 / (1.0 + jnp.exp(-x))


def _softplus(x):
    return jnp.maximum(x, 0.0) + jnp.log(1.0 + jnp.exp(-jnp.abs(x)))


def _rms(x, g):
    return x * lax.rsqrt(jnp.mean(x * x, axis=-1, keepdims=True) + RMS_EPS) * g


def _norm_matmul_kernel(x_ref, g_ref, w_ref, *out_refs, precise):
    h = _rms(x_ref[...], g_ref[...])
    off = 0
    for o_ref in out_refs:
        n = o_ref.shape[-1]
        o_ref[...] = _mm(h, w_ref[:, off:off + n], precise=precise)
        off += n


def _norm_matmul(x, g, w, splits, precise):
    n, d = x.shape
    tm = min(512, n)
    assert n % tm == 0 and sum(splits) == w.shape[1]
    return pl.pallas_call(
        functools.partial(_norm_matmul_kernel, precise=precise),
        grid=(n // tm,),
        in_specs=[pl.BlockSpec((tm, d), lambda i: (i, 0)),
                  pl.BlockSpec((1, d), lambda i: (0, 0)),
                  pl.BlockSpec(w.shape, lambda i: (0, 0))],
        out_specs=[pl.BlockSpec((tm, s), lambda i: (i, 0)) for s in splits],
        out_shape=[jax.ShapeDtypeStruct((n, s), F32) for s in splits],
        compiler_params=_params("parallel"),
        name="norm_matmul",
    )(x, g.reshape(1, d), w)


def _matmul_res_kernel(res_ref, g_ref, *refs, precise):
    xo_ref, ho_ref = refs[-2:]
    acc = res_ref[...]
    for a_ref, w_ref in zip(refs[0:-2:2], refs[1:-2:2]):
        acc = acc + _mm(a_ref[...], w_ref[...], precise=precise)
    xo_ref[...] = acc
    ho_ref[...] = _rms(acc, g_ref[...])


def _matmul_res(res, pairs, g, precise):
    n, d = res.shape
    tm = min(512, n)
    assert n % tm == 0
    in_specs = [pl.BlockSpec((tm, d), lambda i: (i, 0)), pl.BlockSpec((1, d), lambda i: (0, 0))]
    args = [res, g.reshape(1, d)]
    for a, w in pairs:
        in_specs += [pl.BlockSpec((tm, a.shape[1]), lambda i: (i, 0)), pl.BlockSpec(w.shape, lambda i: (0, 0))]
        args += [a, w]
    return pl.pallas_call(
        functools.partial(_matmul_res_kernel, precise=precise),
        grid=(n // tm,),
        in_specs=in_specs,
        out_specs=[pl.BlockSpec((tm, d), lambda i: (i, 0))] * 2,
        out_shape=[jax.ShapeDtypeStruct((n, d), F32)] * 2,
        compiler_params=_params("parallel"),
        name="matmul_res",
    )(*args)


def _block_diag(x, left):
    return jnp.concatenate([jnp.where(left, x, 0.0), jnp.where(left, 0.0, x)], axis=0)


def _rwkv_kernel(zr_ref, shift0_ref, s0_ref, mu_ref, w0_ref, wcomb_ref, a0_ref, wgate_ref, kk_ref, ka_ref,
                 rk_ref, lnw_ref, lnb_ref, e_ref, y_ref, sout_ref, s_scr, prev_scr, *, t_valid, precise):
    c_len = WKV_CHUNK
    n_sb = zr_ref.shape[0]
    c = pl.program_id(1)
    mm = functools.partial(_mm, precise=precise)

    @pl.when(c == 0)
    def _():
        s_scr[...] = s0_ref[...]
        prev_scr[...] = shift0_ref[...]

    row = lax.broadcasted_iota(jnp.int32, (c_len, 1), 0)
    lane = lax.broadcasted_iota(jnp.int32, (1, V7X_LANES), 1)
    left = lane < HD_RWKV
    valid = row < t_valid
    s_le_t = (lane % HD_RWKV) <= row
    s_lt_t = (lane % HD_RWKV) < row
    same_head = (lax.broadcasted_iota(jnp.int32, (V7X_LANES, 1), 0) < HD_RWKV) == left
    n_doublings = int(math.log2(c_len))
    e = e_ref[...]

    def seg_sum(x):
        return _mm(x, e, precise=True)

    def cat(x, y):
        return jnp.concatenate([x, y], axis=0)

    def bd(x):
        return _block_diag(x, left)

    seqs = range(n_sb)
    zr = [zr_ref[sq] for sq in seqs]
    z_prev = [jnp.where(row == 0, prev_scr[sq], pltpu.roll(zr[sq], 1, 0)) for sq in seqs]
    for sq in seqs:
        prev_scr[sq] = zr[sq][c_len - 1:c_len, :]
    zs = [zr[sq] + (z_prev[sq] - zr[sq]) * mu_ref[...] for sq in seqs]
    o3 = 3 * D_RWKV
    da = [zs[sq][:, o3 + GATE_LORA:] for sq in seqs]
    pre = [mm(jnp.where(left, jnp.tanh(x), x), wcomb_ref[...]) for x in da]
    gate = [mm(_sigmoid(zs[sq][:, o3:o3 + GATE_LORA]), wgate_ref[...]) for sq in seqs]

    chains = [(sq, p) for sq in seqs for p in range(N_PAIRS)]
    idx = range(len(chains))
    sls = [slice(p * V7X_LANES, (p + 1) * V7X_LANES) for _, p in chains]

    def cols(sq, p, group):
        lo = group * D_RWKV + p * V7X_LANES
        return zs[sq][:, lo:lo + V7X_LANES]

    r = [cols(sq, p, 0) for sq, p in chains]
    k = [cols(sq, p, 1) for sq, p in chains]
    v = [cols(sq, p, 2) for sq, p in chains]
    lw = [-jnp.exp(-_softplus(-(w0_ref[:, sls[i]] + pre[sq][:, sls[i]])) - 0.5)
          for i, (sq, p) in enumerate(chains)]
    iclr = [_sigmoid(a0_ref[:, sls[i]] + pre[sq][:, D_RWKV + p * V7X_LANES:D_RWKV + (p + 1) * V7X_LANES])
            for i, (sq, p) in enumerate(chains)]
    kk = [k[i] * kk_ref[:, sls[i]] for i in idx]
    kk_ss = [seg_sum(kk[i] * kk[i]) for i in idx]
    kk = [kk[i] * lax.rsqrt(jnp.maximum(kk_ss[i], 1e-24)) for i in idx]
    k = [k[i] * (1.0 + (iclr[i] - 1.0) * ka_ref[:, sls[i]]) for i in idx]
    a = [-kk[i] for i in idx]
    b = [kk[i] * iclr[i] for i in idx]
    if t_valid < c_len:
        lw, a, b, k, v = ([jnp.where(valid, x, 0.0) for x in xs] for xs in (lw, a, b, k, v))

    def cumsum_rows(x):
        sh = 1
        while sh < c_len:
            x = x + jnp.where(row >= sh, pltpu.roll(x, sh, 0), 0.0)
            sh *= 2
        return x

    g = [cumsum_rows(x) for x in lw]
    e_g = [jnp.exp(x) for x in g]
    e_ng = [jnp.exp(-x) for x in g]
    g_end = [x[c_len - 1:c_len, :] for x in e_g]
    a_t = [a[i] * jnp.exp(g[i] - lw[i]) for i in idx]
    b_t = [b[i] * e_ng[i] for i in idx]
    k_t = [k[i] * e_ng[i] for i in idx]
    r_t = [r[i] * e_g[i] for i in idx]

    s = [s_scr[sq, p] for sq, p in chains]
    bd_v = [bd(x) for x in v]
    sc = [mm(cat(a_t[i], r_t[i]), cat(bd(b_t[i]), bd(k_t[i])), _NT) for i in idx]
    a_ab = [jnp.where(s_lt_t, x[:c_len, :V7X_LANES], 0.0) for x in sc]
    a_ak = [jnp.where(s_lt_t, x[:c_len, V7X_LANES:], 0.0) for x in sc]
    a_rb = [jnp.where(s_le_t, x[c_len:, :V7X_LANES], 0.0) for x in sc]
    a_rk = [jnp.where(s_le_t, x[c_len:, V7X_LANES:], 0.0) for x in sc]

    x = [mm(a_t[i], s[i], _NT) + mm(a_ak[i], bd_v[i]) for i in idx]
    lmat = a_ab
    for it in range(n_doublings):
        x = [x[i] + mm(lmat[i], bd(x[i])) for i in idx]
        if it + 1 < n_doublings:
            lmat = [mm(lmat[i], bd(lmat[i])) for i in idx]
    u = x
    y = [mm(r_t[i], s[i], _NT) + mm(a_rb[i], bd(u[i])) + mm(a_rk[i], bd_v[i]) for i in idx]
    s_new = [s[i] * g_end[i] + mm(cat(u[i], v[i]), cat(b_t[i] * g_end[i], k_t[i] * g_end[i]), _TN) for i in idx]
    for i, (sq, p) in enumerate(chains):
        s_scr[sq, p] = jnp.where(same_head, s_new[i], 0.0)

    mean = [seg_sum(x) * (1.0 / HD_RWKV) for x in y]
    d = [y[i] - mean[i] for i in idx]
    var = [seg_sum(x * x) * (1.0 / HD_RWKV) for x in d]
    bonus_rk = [seg_sum(r[i] * k[i] * rk_ref[:, sls[i]]) for i in idx]
    for i, (sq, p) in enumerate(chains):
        yn = d[i] * lax.rsqrt(var[i] + LN_X_EPS) * lnw_ref[:, sls[i]] + lnb_ref[:, sls[i]]
        y_ref[sq, :, sls[i]] = (yn + bonus_rk[i] * v[i]) * gate[sq][:, sls[i]]

    sout_ref[...] = s_scr[...]


def _rwkv_mix(zr, n_seq, t_pad, t_valid, shift0, s0_bd, mix, precise):
    n_chunks = t_pad // WKV_CHUNK
    n_sb = math.gcd(n_seq, WKV_SEQS)
    assert t_pad % WKV_CHUNK == 0 and (t_valid == WKV_CHUNK or n_chunks == 1)
    rowvec = lambda n: pl.BlockSpec((1, n), lambda s, c: (0, 0))
    full = lambda a: pl.BlockSpec(a.shape, lambda s, c: (0,) * a.ndim)
    state_spec = pl.BlockSpec((n_sb, N_PAIRS, V7X_LANES, V7X_LANES), lambda s, c: (s, 0, 0, 0))
    e128 = jnp.kron(jnp.eye(2, dtype=F32), jnp.ones((HD_RWKV, HD_RWKV), F32)).astype(BF16)
    return pl.pallas_call(
        functools.partial(_rwkv_kernel, t_valid=t_valid, precise=precise),
        grid=(n_seq // n_sb, n_chunks),
        in_specs=[pl.BlockSpec((n_sb, WKV_CHUNK, N_SHIFT), lambda s, c: (s, c, 0)),
                  pl.BlockSpec((n_sb, 1, N_SHIFT), lambda s, c: (s, 0, 0)),
                  state_spec,
                  rowvec(N_SHIFT), rowvec(D_RWKV), full(mix["wcomb"]), rowvec(D_RWKV), full(mix["wgate"]),
                  rowvec(D_RWKV), rowvec(D_RWKV), rowvec(D_RWKV), rowvec(D_RWKV), rowvec(D_RWKV), full(e128)],
        out_specs=[pl.BlockSpec((n_sb, WKV_CHUNK, D_RWKV), lambda s, c: (s, c, 0)), state_spec],
        out_shape=[jax.ShapeDtypeStruct((n_seq, t_pad, D_RWKV), F32),
                   jax.ShapeDtypeStruct((n_seq, N_PAIRS, V7X_LANES, V7X_LANES), F32)],
        scratch_shapes=[pltpu.VMEM((n_sb, N_PAIRS, V7X_LANES, V7X_LANES), F32),
                        pltpu.VMEM((n_sb, 1, N_SHIFT), F32)],
        compiler_params=_params("parallel", "arbitrary"),
        name="rwkv_mix",
    )(zr, shift0.reshape(n_seq, 1, N_SHIFT), s0_bd, mix["mu"], mix["w0"], mix["wcomb"], mix["a0"], mix["wgate"],
      mix["k_k"], mix["k_a"], mix["r_k"], mix["ln_w"], mix["ln_b"], e128)


def _state_to_block_diag(s):
    n = s.shape[0]
    s = s.reshape(n, N_PAIRS, 2, HD_RWKV, HD_RWKV)
    eye = jnp.eye(2, dtype=s.dtype)
    bd = jnp.einsum("nphvk,hg->nphvgk", s, eye)
    return bd.reshape(n, N_PAIRS, V7X_LANES, V7X_LANES)


def _state_from_block_diag(bd):
    n = bd.shape[0]
    bd = bd.reshape(n, N_PAIRS, 2, HD_RWKV, 2, HD_RWKV)
    s = jnp.stack([bd[:, :, 0, :, 0, :], bd[:, :, 1, :, 1, :]], axis=2)
    return s.reshape(n, H_RWKV, HD_RWKV, HD_RWKV)


def _pool_kernel(u_ref, hist_ref, w_ref, scale_ref, y_ref, carry_scr, *, pos0, precise):
    t = pl.program_id(1)
    tm = u_ref.shape[0]

    @pl.when(t == 0)
    def _():
        carry_scr[...] = hist_ref[0]

    u = u_ref[...]
    ext = jnp.concatenate([carry_scr[...], u], axis=0)
    carry_scr[...] = ext[tm:, :]
    pos = pos0 + t * tm + lax.broadcasted_iota(jnp.int32, (tm, 1), 0)
    acc = ext
    sums = []
    for win in POOL_WINDOWS:
        acc = acc + pltpu.roll(acc, win // 2, 0)
        sums.append(acc[POOL_CARRY:, :])
    for gi, win in enumerate(POOL_WINDOWS):
        sl = slice(gi * POOL_GROUP, (gi + 1) * POOL_GROUP)
        count = jnp.minimum(pos + 1, win).astype(F32)
        x = sums[gi][:, sl] / count - u[:, sl]
        y_ref[:, sl] = _mm(x, w_ref[gi], precise=precise) * scale_ref[:, sl]


def _pool_mix(u, n_seq, t_len, hist, pos0, pool_w_bf16, pool_scale, precise):
    tm = min(512, t_len)
    assert t_len % tm == 0
    n_t = t_len // tm
    return pl.pallas_call(
        functools.partial(_pool_kernel, pos0=pos0, precise=precise),
        grid=(n_seq, n_t),
        in_specs=[pl.BlockSpec((tm, D_POOL), lambda s, t: (s * n_t + t, 0)),
                  pl.BlockSpec((1, POOL_CARRY, D_POOL), lambda s, t: (s, 0, 0)),
                  pl.BlockSpec(pool_w_bf16.shape, lambda s, t: (0, 0, 0)),
                  pl.BlockSpec((1, D_POOL), lambda s, t: (0, 0))],
        out_specs=pl.BlockSpec((tm, D_POOL), lambda s, t: (s * n_t + t, 0)),
        out_shape=jax.ShapeDtypeStruct((n_seq * t_len, D_POOL), F32),
        scratch_shapes=[pltpu.VMEM((POOL_CARRY, D_POOL), F32)],
        compiler_params=_params("parallel", "arbitrary"),
        name="pool_mix",
    )(u, hist, pool_w_bf16, pool_scale.reshape(1, D_POOL))


def _sub_norm(acc, l, lam, subw, lam_init):
    tq = acc.shape[0] // 2
    o = acc[:tq] / l[:tq] - lam * (acc[tq:] / l[tq:])
    return o * lax.rsqrt(jnp.mean(o * o, axis=-1, keepdims=True) + RMS_EPS) * subw * (1.0 - lam_init)


def _attn_prompt_kernel(lam_ref, q_ref, k_ref, v_ref, subw_ref, o_ref, kb_scr, va_scr, m_scr, acc_scr, *,
                        lam_init, strip, lookahead):
    tq = q_ref.shape[1]
    qi = pl.program_id(2)
    lane = lax.broadcasted_iota(jnp.int32, (1, D_HEAD), 1)

    @pl.when(qi == 0)
    def _():
        kb_scr[...] = k_ref[0].astype(BF16)
        va_scr[:, :D_HEAD] = v_ref[0].astype(BF16)
        va_scr[:, D_HEAD:] = jnp.ones((va_scr.shape[0], D_HEAD), BF16)

    q = q_ref[0] * (HD_ATT ** -0.5)
    qs = jnp.concatenate([jnp.where(lane < HD_ATT, q, 0.0), jnp.where(lane < HD_ATT, 0.0, q)], axis=0).astype(BF16)
    m_scr[...] = jnp.full(m_scr.shape, NEG_INF, F32)
    acc_scr[...] = jnp.zeros(acc_scr.shape, F32)
    n_lane_blocks = tq // V7X_LANES

    strips = [slice(r, r + strip) for r in range(0, 2 * tq, strip)]

    def process(blocks):
        starts = [pl.multiple_of(j * tq, tq) for j, _ in blocks]
        items = [(masked, st, rows) for (_, masked), st in zip(blocks, starts) for rows in strips]

        def score(t):
            _, st, rows = items[t]
            return lax.dot_general(qs[rows], kb_scr[pl.ds(st, tq), :], _NT, preferred_element_type=F32)

        scores = {t: score(t) for t in range(min(lookahead, len(items)))}
        for t, (masked, st, rows) in enumerate(items):
            if t + lookahead < len(items):
                scores[t + lookahead] = score(t + lookahead)
            s = scores.pop(t)
            if masked:
                qpos = (rows.start % tq) + lax.broadcasted_iota(jnp.int32, (strip, 1), 0)
                kpos = lax.broadcasted_iota(jnp.int32, (1, tq), 1)
                s = jnp.where(kpos <= qpos, s, NEG_INF)
            m_old = m_scr[rows, :]
            m_new = jnp.maximum(m_old, jnp.max(s, axis=-1, keepdims=True))
            alpha = jnp.exp(m_old - m_new)
            p = jnp.concatenate([jnp.exp(s[:, c * V7X_LANES:(c + 1) * V7X_LANES] - m_new)
                                 for c in range(n_lane_blocks)], axis=1).astype(BF16)
            pv = jnp.dot(p, va_scr[pl.ds(st, tq), :], preferred_element_type=F32)
            acc_scr[rows, :] = jnp.concatenate([alpha, alpha], axis=1) * acc_scr[rows, :] + pv
            m_scr[rows, :] = m_new

    def body(jj, carry):
        process([(2 * jj, False), (2 * jj + 1, False)])
        return carry

    lax.fori_loop(0, qi // 2, body, 0)

    @pl.when(qi % 2 == 1)
    def _():
        process([(qi - 1, False), (qi, True)])

    @pl.when(qi % 2 == 0)
    def _():
        process([(qi, True)])

    acc = acc_scr[...]
    o_ref[0] = _sub_norm(acc[:, :D_HEAD], acc[:, D_HEAD:], lam_ref[0, 0], subw_ref[...], lam_init)


def _attn_prompt(q, k, v, lam, subw, lam_init, n_batch, t_len):
    tq = min(512, t_len)
    strip = min(ATTN_STRIP, tq)
    assert t_len % tq == 0
    kv_spec = pl.BlockSpec((1, t_len, D_HEAD), lambda b, h, i: (b, 0, h))
    return pl.pallas_call(
        functools.partial(_attn_prompt_kernel, lam_init=lam_init, strip=strip, lookahead=ATTN_LOOKAHEAD),
        grid=(n_batch, H_ATT, t_len // tq),
        in_specs=[pl.BlockSpec(memory_space=pltpu.SMEM),
                  pl.BlockSpec((1, tq, D_HEAD), lambda b, h, i: (b, i, h)),
                  kv_spec, kv_spec,
                  pl.BlockSpec((1, D_HEAD), lambda b, h, i: (0, 0))],
        out_specs=pl.BlockSpec((1, tq, D_HEAD), lambda b, h, i: (b, i, h)),
        out_shape=jax.ShapeDtypeStruct((n_batch, t_len, D_MODEL), F32),
        scratch_shapes=[pltpu.VMEM((t_len, D_HEAD), BF16), pltpu.VMEM((t_len, 2 * D_HEAD), BF16),
                        pltpu.VMEM((2 * tq, V7X_LANES), F32), pltpu.VMEM((2 * tq, 2 * D_HEAD), F32)],
        compiler_params=_params("parallel", "parallel", "arbitrary"),
        name="attn_prompt",
    )(lam.reshape(1, 1), q, k, v, subw.reshape(1, D_HEAD))


def _attn_sample_kernel(pt_ref, lam_ref, q_ref, kn_ref, vn_ref, subw_ref, *refs, lam_init, t_new):
    k_refs = refs[:PAGES_PER_STEP]
    v_refs = refs[PAGES_PER_STEP:2 * PAGES_PER_STEP]
    o_ref, qm_scr, bias_scr, m_scr, l_scr, acc_scr = refs[2 * PAGES_PER_STEP:]
    g = pl.program_id(1)
    n_rows = 2 * H_ATT * t_new
    n_cols = PAGE_SIZE * H_ATT
    row_head = lax.broadcasted_iota(jnp.int32, (n_rows, 1), 0) // (2 * t_new)
    lane = lax.broadcasted_iota(jnp.int32, (1, D_HEAD), 1)

    @pl.when(g == 0)
    def _():
        q = q_ref[0] * (HD_ATT ** -0.5)
        pieces = []
        for h in range(H_ATT):
            qh = q[:, h * D_HEAD:(h + 1) * D_HEAD]
            pieces += [jnp.where(lane < HD_ATT, qh, 0.0), jnp.where(lane < HD_ATT, 0.0, qh)]
        qm_scr[...] = jnp.concatenate(pieces, axis=0).astype(BF16)
        col_head = lax.broadcasted_iota(jnp.int32, (1, n_cols), 1) % H_ATT
        bias_scr[...] = jnp.where(row_head == col_head, 0.0, NEG_INF)
        m_scr[...] = jnp.full(m_scr.shape, NEG_INF, F32)
        l_scr[...] = jnp.zeros(l_scr.shape, F32)
        acc_scr[...] = jnp.zeros(acc_scr.shape, F32)

    def update(scores, values):
        m, l, acc = m_scr[...], l_scr[...], acc_scr[...]
        for s, vmat in zip(scores, values):
            m_new = jnp.maximum(m, jnp.max(s, axis=-1, keepdims=True))
            alpha = jnp.exp(m - m_new)
            p = jnp.concatenate([jnp.exp(s[:, c * V7X_LANES:(c + 1) * V7X_LANES] - m_new)
                                 for c in range(s.shape[1] // V7X_LANES)], axis=1)
            l = alpha * l + jnp.sum(p, axis=-1, keepdims=True)
            acc = alpha * acc + jnp.dot(p.astype(BF16), vmat, preferred_element_type=F32)
            m = m_new
        m_scr[...], l_scr[...], acc_scr[...] = m, l, acc

    qm = qm_scr[...]
    keys = [k_refs[i][0, 0].reshape(n_cols, D_HEAD).astype(BF16) for i in range(PAGES_PER_STEP)]
    values = [v_refs[i][0, 0].reshape(n_cols, D_HEAD).astype(BF16) for i in range(PAGES_PER_STEP)]
    update([lax.dot_general(qm, kf, _NT, preferred_element_type=F32) + bias_scr[...] for kf in keys], values)

    @pl.when(g == pl.num_programs(1) - 1)
    def _():
        n_new = t_new * H_ATT
        pad = jnp.zeros((V7X_LANES - n_new, D_HEAD), BF16)
        kn = jnp.concatenate([kn_ref[0].reshape(n_new, D_HEAD).astype(BF16), pad], axis=0)
        vn = jnp.concatenate([vn_ref[0].reshape(n_new, D_HEAD).astype(BF16), pad], axis=0)
        col = lax.broadcasted_iota(jnp.int32, (1, V7X_LANES), 1)
        qpos = lax.broadcasted_iota(jnp.int32, (n_rows, 1), 0) % t_new
        ok = (col < n_new) & (col % H_ATT == row_head) & (col // H_ATT <= qpos)
        update([jnp.where(ok, lax.dot_general(qm, kn, _NT, preferred_element_type=F32), NEG_INF)], [vn])
        acc = acc_scr[...]
        l = l_scr[...]
        lam = lam_ref[0, 0]
        for h in range(H_ATT):
            rows = slice(2 * h * t_new, 2 * (h + 1) * t_new)
            o_ref[0, :, h * D_HEAD:(h + 1) * D_HEAD] = _sub_norm(acc[rows], l[rows], lam, subw_ref[...], lam_init)


def _attn_sample(q, k_new, v_new, cache_k, cache_v, layer_j, page_table, lam, subw, lam_init):
    n_batch, t_new, _ = q.shape
    n_pages = page_table.shape[1]
    assert n_pages % PAGES_PER_STEP == 0 and t_new <= PAGE_SIZE and t_new % V7X_SUBLANES == 0
    n_steps = n_pages // PAGES_PER_STEP
    n_rows = 2 * H_ATT * t_new
    assert n_rows == V7X_LANES and t_new * H_ATT <= V7X_LANES

    def page_spec(i):
        return pl.BlockSpec((1, 1, PAGE_SIZE, H_ATT, D_HEAD),
                            lambda b, g, pt: (layer_j, pt[b, g * PAGES_PER_STEP + i], 0, 0, 0))

    tok_spec = pl.BlockSpec((1, t_new, D_MODEL), lambda b, g, pt: (b, 0, 0))
    new_spec = pl.BlockSpec((1, t_new, H_ATT, D_HEAD), lambda b, g, pt: (b, 0, 0, 0))
    grid_spec = pltpu.PrefetchScalarGridSpec(
        num_scalar_prefetch=1,
        grid=(n_batch, n_steps),
        in_specs=[pl.BlockSpec(memory_space=pltpu.SMEM), tok_spec, new_spec, new_spec,
                  pl.BlockSpec((1, D_HEAD), lambda b, g, pt: (0, 0))]
                 + [page_spec(i) for i in range(PAGES_PER_STEP)] * 2,
        out_specs=tok_spec,
        scratch_shapes=[pltpu.VMEM((n_rows, D_HEAD), BF16),
                        pltpu.VMEM((n_rows, PAGE_SIZE * H_ATT), F32),
                        pltpu.VMEM((n_rows, V7X_LANES), F32), pltpu.VMEM((n_rows, V7X_LANES), F32),
                        pltpu.VMEM((n_rows, D_HEAD), F32)])
    new_shape = (n_batch, t_new, H_ATT, D_HEAD)
    return pl.pallas_call(
        functools.partial(_attn_sample_kernel, lam_init=lam_init, t_new=t_new),
        grid_spec=grid_spec,
        out_shape=jax.ShapeDtypeStruct((n_batch, t_new, D_MODEL), F32),
        compiler_params=_params("parallel", "arbitrary"),
        name="attn_sample",
    )(page_table, lam.reshape(1, 1), q, k_new.reshape(new_shape), v_new.reshape(new_shape), subw.reshape(1, D_HEAD),
      *([cache_k] * PAGES_PER_STEP), *([cache_v] * PAGES_PER_STEP))


ROUTE_LANES = ("expert0", "expert1", "gate0", "gate1", "rank0", "rank1")


def _router_kernel(h_ref, w_ref, b_ref, o_ref, cnt_ref, carry_scr):
    i = pl.program_id(0)
    tm = h_ref.shape[0]

    @pl.when(i == 0)
    def _():
        carry_scr[...] = jnp.zeros(carry_scr.shape, F32)

    logits = _mm(h_ref[...], w_ref[...], precise=True) + b_ref[...]
    lane = lax.broadcasted_iota(jnp.int32, (1, V7X_LANES), 1)
    lane_f = lane.astype(F32)

    def max_first(mask):
        v = jnp.where(mask, logits, -jnp.inf)
        m = jnp.max(v, axis=-1, keepdims=True)
        return m, jnp.min(jnp.where(v == m, lane_f, float(V7X_LANES)), axis=-1, keepdims=True)

    is_group = lane < N_GROUPS
    m_g, grp = max_first(is_group)
    g_gate = 1.0 / jnp.sum(jnp.where(is_group, jnp.exp(logits - m_g), 0.0), axis=-1, keepdims=True)
    first = N_GROUPS + EXPERTS_PER_GROUP * grp
    in_group = (lane_f >= first) & (lane_f < first + EXPERTS_PER_GROUP)
    v1, l1 = max_first(in_group)
    v2, l2 = max_first(in_group & (lane_f != l1))
    e21 = jnp.exp(v2 - v1)
    gate0 = g_gate / (1.0 + e21)
    gate1 = g_gate * e21 / (1.0 + e21)

    hot0 = lane_f == l1
    hot1 = lane_f == l2
    cnt = hot0.astype(F32) + hot1.astype(F32)
    earlier = lax.broadcasted_iota(jnp.int32, (tm, tm), 0) > lax.broadcasted_iota(jnp.int32, (tm, tm), 1)
    before = carry_scr[...] + jnp.dot(earlier.astype(BF16), cnt.astype(BF16), preferred_element_type=F32)
    rank0 = jnp.sum(jnp.where(hot0, before, 0.0), axis=-1, keepdims=True)
    rank1 = jnp.sum(jnp.where(hot1, before, 0.0), axis=-1, keepdims=True)
    carry_scr[...] = carry_scr[...] + jnp.sum(cnt, axis=0, keepdims=True)
    cnt_ref[...] = carry_scr[...]

    cols = dict(expert0=l1 - N_GROUPS, expert1=l2 - N_GROUPS, gate0=gate0, gate1=gate1, rank0=rank0, rank1=rank1)
    out = jnp.zeros((tm, V7X_LANES), F32)
    for k, name in enumerate(ROUTE_LANES):
        out = jnp.where(lane == k, cols[name], out)
    o_ref[...] = out


def _route(h, w_pad, b_pad):
    n, d = h.shape
    tm = min(512, n)
    assert n % tm == 0
    return pl.pallas_call(
        _router_kernel,
        grid=(n // tm,),
        in_specs=[pl.BlockSpec((tm, d), lambda i: (i, 0)),
                  pl.BlockSpec(w_pad.shape, lambda i: (0, 0)),
                  pl.BlockSpec((1, V7X_LANES), lambda i: (0, 0))],
        out_specs=[pl.BlockSpec((tm, V7X_LANES), lambda i: (i, 0)), pl.BlockSpec((1, V7X_LANES), lambda i: (0, 0))],
        out_shape=[jax.ShapeDtypeStruct((n, V7X_LANES), F32), jax.ShapeDtypeStruct((1, V7X_LANES), F32)],
        scratch_shapes=[pltpu.VMEM((1, V7X_LANES), F32)],
        compiler_params=_params("arbitrary"),
        name="route",
    )(h, w_pad, b_pad)


def _row_copy(src, src_row, dst, dst_row, sem):
    return pltpu.make_async_copy(src.at[pl.ds(src_row, 1), :], dst.at[pl.ds(dst_row, 1), :], sem)


def _scatter_rows_kernel(dest_ref, h_ref, xg_init_ref, xg_ref, sem):
    del xg_init_ref
    n_copies = TOP_K * h_ref.shape[0]

    def start(a, carry):
        _row_copy(h_ref, lax.shift_right_logical(a, 1), xg_ref, dest_ref[0, 0, a], sem).start()
        return carry

    def wait(a, carry):
        _row_copy(h_ref, 0, xg_ref, 0, sem).wait()
        return carry

    lax.fori_loop(0, n_copies, start, 0, unroll=8)
    lax.fori_loop(0, n_copies, wait, 0, unroll=8)


def _scatter_rows(h, dest, n_rows):
    n, d = h.shape
    tm = min(256, n)
    assert n % tm == 0 and TOP_K == 2
    return pl.pallas_call(
        _scatter_rows_kernel,
        grid=(n // tm,),
        in_specs=[pl.BlockSpec((1, 1, TOP_K * tm), lambda i: (i, 0, 0), memory_space=pltpu.SMEM),
                  pl.BlockSpec((tm, d), lambda i: (i, 0)),
                  pl.BlockSpec(memory_space=pl.ANY)],
        out_specs=pl.BlockSpec(memory_space=pl.ANY),
        out_shape=jax.ShapeDtypeStruct((n_rows, d), F32),
        scratch_shapes=[pltpu.SemaphoreType.DMA(())],
        input_output_aliases={2: 0},
        compiler_params=_params("arbitrary"),
        name="scatter_rows",
    )(dest.reshape(n // tm, 1, TOP_K * tm), h, jnp.zeros((n_rows, d), F32))


def _combine_kernel(dest_ref, x_ref, gates_ref, g_ref, y_hbm_ref, o_ref, ybuf, sem, *, normed):
    tm = x_ref.shape[0]

    def start(a, carry):
        _row_copy(y_hbm_ref, dest_ref[0, 0, a], ybuf.at[a & 1], lax.shift_right_logical(a, 1), sem).start()
        return carry

    def wait(a, carry):
        _row_copy(y_hbm_ref, 0, ybuf.at[0], 0, sem).wait()
        return carry

    lax.fori_loop(0, TOP_K * tm, start, 0, unroll=8)
    lax.fori_loop(0, TOP_K * tm, wait, 0, unroll=8)
    gates = gates_ref[...]
    x = x_ref[...] + (ybuf[0] * gates[:, 0:1] + ybuf[1] * gates[:, 1:2])
    o_ref[...] = _rms(x, g_ref[...]) if normed else x


def _combine(x, y_rows, dest, gates, g, normed):
    n, d = x.shape
    tm = min(256, n)
    assert n % tm == 0 and TOP_K == 2
    row = pl.BlockSpec((tm, d), lambda i: (i, 0))
    return pl.pallas_call(
        functools.partial(_combine_kernel, normed=normed),
        grid=(n // tm,),
        in_specs=[pl.BlockSpec((1, 1, TOP_K * tm), lambda i: (i, 0, 0), memory_space=pltpu.SMEM),
                  row, pl.BlockSpec((tm, TOP_K), lambda i: (i, 0)), pl.BlockSpec((1, d), lambda i: (0, 0)),
                  pl.BlockSpec(memory_space=pl.ANY)],
        out_specs=row,
        out_shape=jax.ShapeDtypeStruct((n, d), F32),
        scratch_shapes=[pltpu.VMEM((TOP_K, tm, d), F32), pltpu.SemaphoreType.DMA(())],
        compiler_params=_params("arbitrary"),
        name="combine",
    )(dest.reshape(n // tm, 1, TOP_K * tm), x, gates, g.reshape(1, d), y_rows)


def _expert_kernel(be_ref, bv_ref, x_ref, wg_ref, wu_ref, wd_ref, y_ref, wg_scr, wu_scr, wd_scr):
    i = pl.program_id(0)

    @pl.when((i == 0) | (be_ref[i] != be_ref[jnp.maximum(i - 1, 0)]))
    def _():
        wg_scr[...] = wg_ref[0, 0].astype(BF16)
        wu_scr[...] = wu_ref[0, 0].astype(BF16)
        wd_scr[...] = wd_ref[0, 0].astype(BF16)

    @pl.when(bv_ref[i] > 0)
    def _():
        x = x_ref[...].astype(BF16)
        hg = jnp.dot(x, wg_scr[...], preferred_element_type=F32)
        hu = jnp.dot(x, wu_scr[...], preferred_element_type=F32)
        hdn = (hg * _sigmoid(hg)) * hu
        y_ref[...] = jnp.dot(hdn.astype(BF16), wd_scr[...], preferred_element_type=F32)

    @pl.when(bv_ref[i] == 0)
    def _():
        y_ref[...] = jnp.zeros(y_ref.shape, F32)


def _expert_blocks(xg, block_expert, block_valid, wg, wu, wd, layer):
    n_rows, d = xg.shape
    n_blocks = n_rows // MOE_ROWS
    grid_spec = pltpu.PrefetchScalarGridSpec(
        num_scalar_prefetch=2,
        grid=(n_blocks,),
        in_specs=[pl.BlockSpec((MOE_ROWS, d), lambda i, be, bv: (i, 0)),
                  pl.BlockSpec((1, 1, d, D_EXPERT), lambda i, be, bv: (layer, be[i], 0, 0)),
                  pl.BlockSpec((1, 1, d, D_EXPERT), lambda i, be, bv: (layer, be[i], 0, 0)),
                  pl.BlockSpec((1, 1, D_EXPERT, d), lambda i, be, bv: (layer, be[i], 0, 0))],
        out_specs=pl.BlockSpec((MOE_ROWS, d), lambda i, be, bv: (i, 0)),
        scratch_shapes=[pltpu.VMEM((d, D_EXPERT), BF16), pltpu.VMEM((d, D_EXPERT), BF16),
                        pltpu.VMEM((D_EXPERT, d), BF16)])
    return pl.pallas_call(
        _expert_kernel,
        grid_spec=grid_spec,
        out_shape=jax.ShapeDtypeStruct((n_rows, d), F32),
        compiler_params=_params("arbitrary"),
        name="expert_blocks",
    )(block_expert, block_valid, xg, wg, wu, wd)


def _hier_moe(h, moe):
    n, d = h.shape
    routed, lane_counts = _route(h, moe["w_router"], moe["b_router"])
    col = {name: routed[:, k] for k, name in enumerate(ROUTE_LANES)}
    eidx = jnp.stack([col["expert0"], col["expert1"]], axis=1).astype(jnp.int32)
    rank = jnp.stack([col["rank0"], col["rank1"]], axis=1).astype(jnp.int32)
    gates = jnp.stack([col["gate0"], col["gate1"]], axis=1)
    counts = lane_counts[0, N_GROUPS:N_GROUPS + N_EXPERTS].astype(jnp.int32)

    padded = (counts + MOE_ROWS - 1) // MOE_ROWS * MOE_ROWS
    pad_end = jnp.cumsum(padded)
    pad_start = pad_end - padded
    experts = jnp.arange(N_EXPERTS, dtype=jnp.int32)
    dest = jnp.sum(jnp.where(eidx[:, :, None] == experts, pad_start, 0), axis=-1) + rank
    n_blocks = -(-n * TOP_K // MOE_ROWS) + N_EXPERTS
    block_start = jnp.arange(n_blocks, dtype=jnp.int32) * MOE_ROWS
    block_expert = jnp.sum((block_start[:, None] >= pad_end[None, :]).astype(jnp.int32), axis=1)
    block_expert = jnp.minimum(block_expert, N_EXPERTS - 1)
    block_valid = (block_start < pad_end[-1]).astype(jnp.int32)
    xg = _scatter_rows(h, dest, n_blocks * MOE_ROWS)
    y_rows = _expert_blocks(xg, block_expert, block_valid, moe["wg"], moe["wu"], moe["wd"], moe["layer"])
    return y_rows, dest, gates


def _mix_params(j, precise, w_in_mix, mu_shift, w_decay0, w_decay_up, w_iclr0, w_iclr_up, w_gate_up,
                k_k, k_a, r_k, ln_x_w, ln_x_b, pool_w, pool_scale, w_out_mix):
    wdt = F32 if precise else BF16
    zero = jnp.zeros((DECAY_LORA, D_RWKV), F32)
    vec = lambda a: a[j].reshape(1, -1)
    return dict(
        w_in=w_in_mix[j].astype(wdt), mu=vec(mu_shift), w0=vec(w_decay0),
        wcomb=jnp.block([[w_decay_up[j], zero], [zero, w_iclr_up[j]]]).astype(wdt),
        a0=vec(w_iclr0), wgate=w_gate_up[j].astype(wdt), k_k=vec(k_k), k_a=vec(k_a), r_k=vec(r_k),
        ln_w=vec(ln_x_w), ln_b=vec(ln_x_b), pool_w=pool_w[j].astype(wdt), pool_scale=pool_scale[j],
        w_out_a=w_out_mix[j, :D_RWKV].astype(wdt), w_out_b=w_out_mix[j, D_RWKV:].astype(wdt))


def _mixer_group(x, n_seq, t_len, pos0, shift0, wkv0, pool0, mix, norm_g, norm_ffn_g, precise):
    zr, u = _norm_matmul(x, norm_g, mix["w_in"], (N_SHIFT, D_POOL), precise)
    zr = zr.reshape(n_seq, t_len, N_SHIFT)
    t_pad = -(-t_len // WKV_CHUNK) * WKV_CHUNK
    zr_pad = jnp.pad(zr, ((0, 0), (0, t_pad - t_len), (0, 0))) if t_pad != t_len else zr
    y_rwkv, s_bd = _rwkv_mix(zr_pad, n_seq, t_pad, min(t_len, WKV_CHUNK), shift0, _state_to_block_diag(wkv0),
                             mix, precise)
    y_rwkv = y_rwkv[:, :t_len].reshape(n_seq * t_len, D_RWKV)
    hist = jnp.pad(pool0, ((0, 0), (POOL_CARRY - POOL_BUF, 0), (0, 0)))
    y_pool = _pool_mix(u, n_seq, t_len, hist, pos0, mix["pool_w"], mix["pool_scale"], precise)
    x, h = _matmul_res(x, [(y_rwkv, mix["w_out_a"]), (y_pool, mix["w_out_b"])], norm_ffn_g, precise)
    ext = jnp.concatenate([pool0, u.reshape(n_seq, t_len, D_POOL)], axis=1)
    return x, h, zr[:, -1], _state_from_block_diag(s_bd), ext[:, -POOL_BUF:]


def kernel(x_prompt, x_sample, state_wkv, state_shift, state_pool, cache_k, cache_v, page_table,
           norm_mix, norm_ffn, norm_final,
           w_in_mix, mu_shift, w_decay0, w_decay_up, w_iclr0, w_iclr_up, w_gate_up,
           k_k, k_a, r_k, ln_x_w, ln_x_b, pool_w, pool_scale, w_out_mix,
           w_qkv, lam_q1, lam_k1, lam_q2, lam_k2, subln_w, w_o,
           router_group, router_group_bias, router_expert, router_expert_bias,
           w_exp_gate, w_exp_up, w_exp_down):
    bp, tp, d = x_prompt.shape
    bs, ts, _ = x_sample.shape
    depth = norm_mix.shape[0]
    past_len = page_table.shape[1] * PAGE_SIZE
    groups = [dict(x=x_prompt.reshape(bp * tp, d), n_seq=bp, t=tp, precise=False),
              dict(x=x_sample.reshape(bs * ts, d), n_seq=bs, t=ts, precise=True)]
    outs = [dict(wkv=[], shift=[], pool=[], k=[], v=[]) for _ in groups]

    for layer in range(depth):
        j = layer // 2
        if layer % 2 == 0:
            mix_w = (w_in_mix, mu_shift, w_decay0, w_decay_up, w_iclr0, w_iclr_up, w_gate_up,
                     k_k, k_a, r_k, ln_x_w, ln_x_b, pool_w, pool_scale, w_out_mix)
            inits = [(0, jnp.zeros((bp, N_SHIFT), F32), jnp.zeros((bp, H_RWKV, HD_RWKV, HD_RWKV), F32),
                      jnp.zeros((bp, POOL_BUF, D_POOL), F32)),
                     (past_len, state_shift[j], state_wkv[j], state_pool[j])]
            for grp, out, (pos0, sh0, wk0, pl0) in zip(groups, outs, inits):
                mix = _mix_params(j, grp["precise"], *mix_w)
                grp["x"], grp["h"], sh, wk, po = _mixer_group(grp["x"], grp["n_seq"], grp["t"], pos0, sh0, wk0, pl0,
                                                              mix, norm_mix[layer], norm_ffn[layer], grp["precise"])
                out["wkv"].append(wk); out["shift"].append(sh); out["pool"].append(po)
        else:
            lam_init = 0.8 - 0.6 * math.exp(-0.3 * layer)
            lam = (jnp.exp(jnp.sum(lam_q1[j] * lam_k1[j])) - jnp.exp(jnp.sum(lam_q2[j] * lam_k2[j])) + lam_init)
            for gi, (grp, out) in enumerate(zip(groups, outs)):
                n_seq, t, precise = grp["n_seq"], grp["t"], grp["precise"]
                wdt = F32 if precise else BF16
                q, k, v = _norm_matmul(grp["x"], norm_mix[layer], w_qkv[j].astype(wdt), (d, d, d), precise)
                q3, k3, v3 = (a.reshape(n_seq, t, d) for a in (q, k, v))
                if gi == 0:
                    o = _attn_prompt(q3, k3, v3, lam, subln_w[j], lam_init, n_seq, t)
                else:
                    o = _attn_sample(q3, k3, v3, cache_k, cache_v, j, page_table, lam, subln_w[j], lam_init)
                grp["x"], grp["h"] = _matmul_res(grp["x"], [(o.reshape(n_seq * t, d), w_o[j].astype(wdt))],
                                                 norm_ffn[layer], precise)
                out["k"].append(k.reshape(n_seq, t, H_ATT, D_HEAD)); out["v"].append(v.reshape(n_seq, t, H_ATT, D_HEAD))

        n_router = N_GROUPS + N_EXPERTS
        moe = dict(
            w_router=jnp.pad(jnp.concatenate([router_group[layer], router_expert[layer]], axis=1),
                             ((0, 0), (0, V7X_LANES - n_router))),
            b_router=jnp.pad(jnp.concatenate([router_group_bias[layer], router_expert_bias[layer]]),
                             (0, V7X_LANES - n_router)).reshape(1, V7X_LANES),
            wg=w_exp_gate, wu=w_exp_up, wd=w_exp_down, layer=layer)
        for grp in groups:
            y_rows, dest, gates = _hier_moe(grp["h"], moe)
            grp["x"] = _combine(grp["x"], y_rows, dest, gates, norm_final, normed=(layer == depth - 1))

    res = []
    for grp, out in zip(groups, outs):
        res.append([grp["x"].reshape(grp["n_seq"], grp["t"], d), jnp.stack(out["wkv"]), jnp.stack(out["shift"]),
                    jnp.stack(out["pool"]), jnp.stack(out["k"]), jnp.stack(out["v"])])
    (yp, wkp, shp, pop, kp, vp), (ys, wks, shs, pos_, ks_, vs_) = res
    return (yp, ys, wkp, shp, pop, kp, vp, wks, shs, pos_, ks_, vs_)
```

```python
import functools
import math

import jax
import jax.numpy as jnp
from jax import lax
from jax.experimental import pallas as pl
from jax.experimental.pallas import tpu as pltpu

F32 = jnp.float32
BF16 = jnp.bfloat16

V7X_LANES = 128
V7X_SUBLANES = 8
V7X_VMEM_LIMIT_BYTES = 56 * 1024 * 1024

D_MODEL = 1024
RMS_EPS = 1e-5
NEG_INF = -1e30
D_RWKV = 512
HD_RWKV = 64
H_RWKV = 8
N_PAIRS = H_RWKV // 2
DECAY_LORA = 64
ICLR_LORA = 64
GATE_LORA = 128
LN_X_EPS = 64e-5
D_POOL = 512
POOL_WINDOWS = (2, 4, 8, 16)
POOL_GROUP = 128
POOL_BUF = 15
POOL_CARRY = 16
N_SHIFT = 3 * D_RWKV + GATE_LORA + DECAY_LORA + ICLR_LORA
HD_ATT = 64
H_ATT = 8
D_HEAD = 2 * HD_ATT
PAGE_SIZE = 128
N_GROUPS = 4
EXPERTS_PER_GROUP = 8
N_EXPERTS = 32
TOP_K = 2
D_EXPERT = 512
WKV_CHUNK = 64
WKV_SEQS = 4
MOE_ROWS = 256
PAGES_PER_STEP = 16
ATTN_STRIP = 128
ATTN_BLOCKS = 4


def _params(*semantics):
    return pltpu.CompilerParams(dimension_semantics=semantics, vmem_limit_bytes=V7X_VMEM_LIMIT_BYTES)


_NN = (((1,), (0,)), ((), ()))
_NT = (((1,), (1,)), ((), ()))
_TN = (((0,), (0,)), ((), ()))


def _hi_lo(x):
    hi = x.astype(BF16)
    return hi, (None if x.dtype == BF16 else (x - hi.astype(F32)).astype(BF16))


def _mm(a, b, dims=_NN, precise=False):
    dot = functools.partial(lax.dot_general, dimension_numbers=dims, preferred_element_type=F32)
    if not precise:
        return dot(a.astype(BF16), b.astype(BF16))
    a_hi, a_lo = _hi_lo(a)
    b_hi, b_lo = _hi_lo(b)
    out = dot(a_hi, b_hi)
    if b_lo is not None:
        out = out + dot(a_hi, b_lo)
    if a_lo is not None:
        out = out + dot(a_lo, b_hi)
    return out


def _sigmoid(x):
    return 1.0 / (1.0 + jnp.exp(-x))


def _softplus(x):
    return jnp.maximum(x, 0.0) + jnp.log(1.0 + jnp.exp(-jnp.abs(x)))


def _rms(x, g):
    return x * lax.rsqrt(jnp.mean(x * x, axis=-1, keepdims=True) + RMS_EPS) * g


def _norm_matmul_kernel(x_ref, g_ref, w_ref, *out_refs, precise):
    h = _rms(x_ref[...], g_ref[...])
    off = 0
    for o_ref in out_refs:
        n = o_ref.shape[-1]
        o_ref[...] = _mm(h, w_ref[:, off:off + n], precise=precise)
        off += n


def _norm_matmul(x, g, w, splits, precise):
    n, d = x.shape
    tm = min(512, n)
    assert n % tm == 0 and sum(splits) == w.shape[1]
    return pl.pallas_call(
        functools.partial(_norm_matmul_kernel, precise=precise),
        grid=(n // tm,),
        in_specs=[pl.BlockSpec((tm, d), lambda i: (i, 0)),
                  pl.BlockSpec((1, d), lambda i: (0, 0)),
                  pl.BlockSpec(w.shape, lambda i: (0, 0))],
        out_specs=[pl.BlockSpec((tm, s), lambda i: (i, 0)) for s in splits],
        out_shape=[jax.ShapeDtypeStruct((n, s), F32) for s in splits],
        compiler_params=_params("parallel"),
        name="norm_matmul",
    )(x, g.reshape(1, d), w)


def _matmul_res_kernel(res_ref, g_ref, *refs, precise):
    xo_ref, ho_ref = refs[-2:]
    acc = res_ref[...]
    for a_ref, w_ref in zip(refs[0:-2:2], refs[1:-2:2]):
        acc = acc + _mm(a_ref[...], w_ref[...], precise=precise)
    xo_ref[...] = acc
    ho_ref[...] = _rms(acc, g_ref[...])


def _matmul_res(res, pairs, g, precise):
    n, d = res.shape
    tm = min(512, n)
    assert n % tm == 0
    in_specs = [pl.BlockSpec((tm, d), lambda i: (i, 0)), pl.BlockSpec((1, d), lambda i: (0, 0))]
    args = [res, g.reshape(1, d)]
    for a, w in pairs:
        in_specs += [pl.BlockSpec((tm, a.shape[1]), lambda i: (i, 0)), pl.BlockSpec(w.shape, lambda i: (0, 0))]
        args += [a, w]
    return pl.pallas_call(
        functools.partial(_matmul_res_kernel, precise=precise),
        grid=(n // tm,),
        in_specs=in_specs,
        out_specs=[pl.BlockSpec((tm, d), lambda i: (i, 0))] * 2,
        out_shape=[jax.ShapeDtypeStruct((n, d), F32)] * 2,
        compiler_params=_params("parallel"),
        name="matmul_res",
    )(*args)


def _block_diag(x, left):
    return jnp.concatenate([jnp.where(left, x, 0.0), jnp.where(left, 0.0, x)], axis=0)


def _rwkv_kernel(zr_ref, shift0_ref, s0_ref, mu_ref, w0_ref, wcomb_ref, a0_ref, wgate_ref, kk_ref, ka_ref,
                 rk_ref, lnw_ref, lnb_ref, e_ref, y_ref, sout_ref, s_scr, prev_scr, *, t_valid, precise):
    c_len = WKV_CHUNK
    n_sb = zr_ref.shape[0]
    c = pl.program_id(1)
    mm = functools.partial(_mm, precise=precise)

    @pl.when(c == 0)
    def _():
        s_scr[...] = s0_ref[...]
        prev_scr[...] = shift0_ref[...]

    row = lax.broadcasted_iota(jnp.int32, (c_len, 1), 0)
    lane = lax.broadcasted_iota(jnp.int32, (1, V7X_LANES), 1)
    left = lane < HD_RWKV
    valid = row < t_valid
    s_le_t = (lane % HD_RWKV) <= row
    s_lt_t = (lane % HD_RWKV) < row
    same_head = (lax.broadcasted_iota(jnp.int32, (V7X_LANES, 1), 0) < HD_RWKV) == left
    n_doublings = int(math.log2(c_len))
    e = e_ref[...]

    def seg_sum(x):
        return _mm(x, e, precise=True)

    def cat(x, y):
        return jnp.concatenate([x, y], axis=0)

    def bd(x):
        return _block_diag(x, left)

    seqs = range(n_sb)
    zr = [zr_ref[sq] for sq in seqs]
    z_prev = [jnp.where(row == 0, prev_scr[sq], pltpu.roll(zr[sq], 1, 0)) for sq in seqs]
    for sq in seqs:
        prev_scr[sq] = zr[sq][c_len - 1:c_len, :]
    zs = [zr[sq] + (z_prev[sq] - zr[sq]) * mu_ref[...] for sq in seqs]
    o3 = 3 * D_RWKV
    da = [zs[sq][:, o3 + GATE_LORA:] for sq in seqs]
    pre = [mm(jnp.where(left, jnp.tanh(x), x), wcomb_ref[...]) for x in da]
    gate = [mm(_sigmoid(zs[sq][:, o3:o3 + GATE_LORA]), wgate_ref[...]) for sq in seqs]

    chains = [(sq, p) for sq in seqs for p in range(N_PAIRS)]
    idx = range(len(chains))
    sls = [slice(p * V7X_LANES, (p + 1) * V7X_LANES) for _, p in chains]

    def cols(sq, p, group):
        lo = group * D_RWKV + p * V7X_LANES
        return zs[sq][:, lo:lo + V7X_LANES]

    r = [cols(sq, p, 0) for sq, p in chains]
    k = [cols(sq, p, 1) for sq, p in chains]
    v = [cols(sq, p, 2) for sq, p in chains]
    lw = [-jnp.exp(-_softplus(-(w0_ref[:, sls[i]] + pre[sq][:, sls[i]])) - 0.5)
          for i, (sq, p) in enumerate(chains)]
    iclr = [_sigmoid(a0_ref[:, sls[i]] + pre[sq][:, D_RWKV + p * V7X_LANES:D_RWKV + (p + 1) * V7X_LANES])
            for i, (sq, p) in enumerate(chains)]
    kk = [k[i] * kk_ref[:, sls[i]] for i in idx]
    kk_ss = [seg_sum(kk[i] * kk[i]) for i in idx]
    kk = [kk[i] * lax.rsqrt(jnp.maximum(kk_ss[i], 1e-24)) for i in idx]
    k = [k[i] * (1.0 + (iclr[i] - 1.0) * ka_ref[:, sls[i]]) for i in idx]
    a = [-kk[i] for i in idx]
    b = [kk[i] * iclr[i] for i in idx]
    if t_valid < c_len:
        lw, a, b, k, v = ([jnp.where(valid, x, 0.0) for x in xs] for xs in (lw, a, b, k, v))

    def cumsum_rows(x):
        sh = 1
        while sh < c_len:
            x = x + jnp.where(row >= sh, pltpu.roll(x, sh, 0), 0.0)
            sh *= 2
        return x

    g = [cumsum_rows(x) for x in lw]
    e_g = [jnp.exp(x) for x in g]
    e_ng = [jnp.exp(-x) for x in g]
    g_end = [x[c_len - 1:c_len, :] for x in e_g]
    a_t = [a[i] * jnp.exp(g[i] - lw[i]) for i in idx]
    b_t = [b[i] * e_ng[i] for i in idx]
    k_t = [k[i] * e_ng[i] for i in idx]
    r_t = [r[i] * e_g[i] for i in idx]

    s = [s_scr[sq, p] for sq, p in chains]
    bd_v = [bd(x) for x in v]
    sc = [mm(cat(a_t[i], r_t[i]), cat(bd(b_t[i]), bd(k_t[i])), _NT) for i in idx]
    a_ab = [jnp.where(s_lt_t, x[:c_len, :V7X_LANES], 0.0) for x in sc]
    a_ak = [jnp.where(s_lt_t, x[:c_len, V7X_LANES:], 0.0) for x in sc]
    a_rb = [jnp.where(s_le_t, x[c_len:, :V7X_LANES], 0.0) for x in sc]
    a_rk = [jnp.where(s_le_t, x[c_len:, V7X_LANES:], 0.0) for x in sc]

    x = [mm(a_t[i], s[i], _NT) + mm(a_ak[i], bd_v[i]) for i in idx]
    lmat = a_ab
    for it in range(n_doublings):
        x = [x[i] + mm(lmat[i], bd(x[i])) for i in idx]
        if it + 1 < n_doublings:
            lmat = [mm(lmat[i], bd(lmat[i])) for i in idx]
    u = x
    y = [mm(r_t[i], s[i], _NT) + mm(a_rb[i], bd(u[i])) + mm(a_rk[i], bd_v[i]) for i in idx]
    s_new = [s[i] * g_end[i] + mm(cat(u[i], v[i]), cat(b_t[i] * g_end[i], k_t[i] * g_end[i]), _TN) for i in idx]
    for i, (sq, p) in enumerate(chains):
        s_scr[sq, p] = jnp.where(same_head, s_new[i], 0.0)

    mean = [seg_sum(x) * (1.0 / HD_RWKV) for x in y]
    d = [y[i] - mean[i] for i in idx]
    var = [seg_sum(x * x) * (1.0 / HD_RWKV) for x in d]
    bonus_rk = [seg_sum(r[i] * k[i] * rk_ref[:, sls[i]]) for i in idx]
    for i, (sq, p) in enumerate(chains):
        yn = d[i] * lax.rsqrt(var[i] + LN_X_EPS) * lnw_ref[:, sls[i]] + lnb_ref[:, sls[i]]
        y_ref[sq, :, sls[i]] = (yn + bonus_rk[i] * v[i]) * gate[sq][:, sls[i]]

    sout_ref[...] = s_scr[...]


def _rwkv_mix(zr, n_seq, t_pad, t_valid, shift0, s0_bd, mix, precise):
    n_chunks = t_pad // WKV_CHUNK
    n_sb = math.gcd(n_seq, WKV_SEQS)
    assert t_pad % WKV_CHUNK == 0 and (t_valid == WKV_CHUNK or n_chunks == 1)
    rowvec = lambda n: pl.BlockSpec((1, n), lambda s, c: (0, 0))
    full = lambda a: pl.BlockSpec(a.shape, lambda s, c: (0,) * a.ndim)
    state_spec = pl.BlockSpec((n_sb, N_PAIRS, V7X_LANES, V7X_LANES), lambda s, c: (s, 0, 0, 0))
    e128 = jnp.kron(jnp.eye(2, dtype=F32), jnp.ones((HD_RWKV, HD_RWKV), F32)).astype(BF16)
    return pl.pallas_call(
        functools.partial(_rwkv_kernel, t_valid=t_valid, precise=precise),
        grid=(n_seq // n_sb, n_chunks),
        in_specs=[pl.BlockSpec((n_sb, WKV_CHUNK, N_SHIFT), lambda s, c: (s, c, 0)),
                  pl.BlockSpec((n_sb, 1, N_SHIFT), lambda s, c: (s, 0, 0)),
                  state_spec,
                  rowvec(N_SHIFT), rowvec(D_RWKV), full(mix["wcomb"]), rowvec(D_RWKV), full(mix["wgate"]),
                  rowvec(D_RWKV), rowvec(D_RWKV), rowvec(D_RWKV), rowvec(D_RWKV), rowvec(D_RWKV), full(e128)],
        out_specs=[pl.BlockSpec((n_sb, WKV_CHUNK, D_RWKV), lambda s, c: (s, c, 0)), state_spec],
        out_shape=[jax.ShapeDtypeStruct((n_seq, t_pad, D_RWKV), F32),
                   jax.ShapeDtypeStruct((n_seq, N_PAIRS, V7X_LANES, V7X_LANES), F32)],
        scratch_shapes=[pltpu.VMEM((n_sb, N_PAIRS, V7X_LANES, V7X_LANES), F32),
                        pltpu.VMEM((n_sb, 1, N_SHIFT), F32)],
        compiler_params=_params("parallel", "arbitrary"),
        name="rwkv_mix",
    )(zr, shift0.reshape(n_seq, 1, N_SHIFT), s0_bd, mix["mu"], mix["w0"], mix["wcomb"], mix["a0"], mix["wgate"],
      mix["k_k"], mix["k_a"], mix["r_k"], mix["ln_w"], mix["ln_b"], e128)


def _state_to_block_diag(s):
    n = s.shape[0]
    s = s.reshape(n, N_PAIRS, 2, HD_RWKV, HD_RWKV)
    eye = jnp.eye(2, dtype=s.dtype)
    bd = jnp.einsum("nphvk,hg->nphvgk", s, eye)
    return bd.reshape(n, N_PAIRS, V7X_LANES, V7X_LANES)


def _state_from_block_diag(bd):
    n = bd.shape[0]
    bd = bd.reshape(n, N_PAIRS, 2, HD_RWKV, 2, HD_RWKV)
    s = jnp.stack([bd[:, :, 0, :, 0, :], bd[:, :, 1, :, 1, :]], axis=2)
    return s.reshape(n, H_RWKV, HD_RWKV, HD_RWKV)


def _pool_kernel(u_ref, hist_ref, w_ref, scale_ref, y_ref, carry_scr, *, pos0, precise):
    t = pl.program_id(1)
    tm = u_ref.shape[0]

    @pl.when(t == 0)
    def _():
        carry_scr[...] = hist_ref[0]

    u = u_ref[...]
    ext = jnp.concatenate([carry_scr[...], u], axis=0)
    carry_scr[...] = ext[tm:, :]
    pos = pos0 + t * tm + lax.broadcasted_iota(jnp.int32, (tm, 1), 0)
    acc = ext
    sums = []
    for win in POOL_WINDOWS:
        acc = acc + pltpu.roll(acc, win // 2, 0)
        sums.append(acc[POOL_CARRY:, :])
    for gi, win in enumerate(POOL_WINDOWS):
        sl = slice(gi * POOL_GROUP, (gi + 1) * POOL_GROUP)
        count = jnp.minimum(pos + 1, win).astype(F32)
        x = sums[gi][:, sl] / count - u[:, sl]
        y_ref[:, sl] = _mm(x, w_ref[gi], precise=precise) * scale_ref[:, sl]


def _pool_mix(u, n_seq, t_len, hist, pos0, pool_w_bf16, pool_scale, precise):
    tm = min(512, t_len)
    assert t_len % tm == 0
    n_t = t_len // tm
    return pl.pallas_call(
        functools.partial(_pool_kernel, pos0=pos0, precise=precise),
        grid=(n_seq, n_t),
        in_specs=[pl.BlockSpec((tm, D_POOL), lambda s, t: (s * n_t + t, 0)),
                  pl.BlockSpec((1, POOL_CARRY, D_POOL), lambda s, t: (s, 0, 0)),
                  pl.BlockSpec(pool_w_bf16.shape, lambda s, t: (0, 0, 0)),
                  pl.BlockSpec((1, D_POOL), lambda s, t: (0, 0))],
        out_specs=pl.BlockSpec((tm, D_POOL), lambda s, t: (s * n_t + t, 0)),
        out_shape=jax.ShapeDtypeStruct((n_seq * t_len, D_POOL), F32),
        scratch_shapes=[pltpu.VMEM((POOL_CARRY, D_POOL), F32)],
        compiler_params=_params("parallel", "arbitrary"),
        name="pool_mix",
    )(u, hist, pool_w_bf16, pool_scale.reshape(1, D_POOL))


def _sub_norm(acc, l, lam, subw, lam_init):
    tq = acc.shape[0] // 2
    o = acc[:tq] / l[:tq] - lam * (acc[tq:] / l[tq:])
    return o * lax.rsqrt(jnp.mean(o * o, axis=-1, keepdims=True) + RMS_EPS) * subw * (1.0 - lam_init)


def _attn_prompt_kernel(lam_ref, q_ref, k_ref, v_ref, subw_ref, o_ref, kb_scr, va_scr, m_scr, acc_scr, *,
                        lam_init, strip):
    tq = q_ref.shape[1]
    qi = pl.program_id(2)
    lane = lax.broadcasted_iota(jnp.int32, (1, D_HEAD), 1)

    @pl.when(qi == 0)
    def _():
        kb_scr[...] = k_ref[0].astype(BF16)
        va_scr[:, :D_HEAD] = v_ref[0].astype(BF16)
        va_scr[:, D_HEAD:] = jnp.ones((va_scr.shape[0], D_HEAD), BF16)

    q = q_ref[0] * (HD_ATT ** -0.5)
    qs = jnp.concatenate([jnp.where(lane < HD_ATT, q, 0.0), jnp.where(lane < HD_ATT, 0.0, q)], axis=0).astype(BF16)
    m_scr[...] = jnp.full(m_scr.shape, NEG_INF, F32)
    acc_scr[...] = jnp.zeros(acc_scr.shape, F32)

    strips = [slice(r, r + strip) for r in range(0, 2 * tq, strip)]

    def process(blocks):
        starts = [pl.multiple_of(j * tq, tq) for j, _ in blocks]
        items = [(masked, st, rows, (rows.start % tq + strip) if masked else tq)
                 for (_, masked), st in zip(blocks, starts) for rows in strips]
        scores = [lax.dot_general(qs[rows], kb_scr[pl.ds(st, n_keys), :], _NT, preferred_element_type=F32)
                  for _, st, rows, n_keys in items]
        for (masked, st, rows, n_keys), s in zip(items, scores):
            if masked:
                qpos = (rows.start % tq) + lax.broadcasted_iota(jnp.int32, (strip, 1), 0)
                kpos = lax.broadcasted_iota(jnp.int32, (1, n_keys), 1)
                s = jnp.where(kpos <= qpos, s, NEG_INF)
            m_old = m_scr[rows, :]
            m_new = jnp.maximum(m_old, jnp.max(s, axis=-1, keepdims=True))
            alpha = jnp.exp(m_old - m_new)
            p = jnp.concatenate([jnp.exp(s[:, c * V7X_LANES:(c + 1) * V7X_LANES] - m_new)
                                 for c in range(n_keys // V7X_LANES)], axis=1).astype(BF16)
            pv = jnp.dot(p, va_scr[pl.ds(st, n_keys), :], preferred_element_type=F32)
            acc_scr[rows, :] = jnp.concatenate([alpha, alpha], axis=1) * acc_scr[rows, :] + pv
            m_scr[rows, :] = m_new

    def body(jj, carry):
        process([(ATTN_BLOCKS * jj + b, False) for b in range(ATTN_BLOCKS)])
        return carry

    lax.fori_loop(0, qi // ATTN_BLOCKS, body, 0)
    for rest in range(ATTN_BLOCKS):
        @pl.when(qi % ATTN_BLOCKS == rest)
        def _(rest=rest):
            process([(qi - rest + b, False) for b in range(rest)] + [(qi, True)])

    acc = acc_scr[...]
    o_ref[0] = _sub_norm(acc[:, :D_HEAD], acc[:, D_HEAD:], lam_ref[0, 0], subw_ref[...], lam_init)


def _attn_prompt(q, k, v, lam, subw, lam_init, n_batch, t_len):
    tq = min(512, t_len)
    strip = min(ATTN_STRIP, tq)
    assert t_len % tq == 0
    kv_spec = pl.BlockSpec((1, t_len, D_HEAD), lambda b, h, i: (b, 0, h))
    return pl.pallas_call(
        functools.partial(_attn_prompt_kernel, lam_init=lam_init, strip=strip),
        grid=(n_batch, H_ATT, t_len // tq),
        in_specs=[pl.BlockSpec(memory_space=pltpu.SMEM),
                  pl.BlockSpec((1, tq, D_HEAD), lambda b, h, i: (b, i, h)),
                  kv_spec, kv_spec,
                  pl.BlockSpec((1, D_HEAD), lambda b, h, i: (0, 0))],
        out_specs=pl.BlockSpec((1, tq, D_HEAD), lambda b, h, i: (b, i, h)),
        out_shape=jax.ShapeDtypeStruct((n_batch, t_len, D_MODEL), F32),
        scratch_shapes=[pltpu.VMEM((t_len, D_HEAD), BF16), pltpu.VMEM((t_len, 2 * D_HEAD), BF16),
                        pltpu.VMEM((2 * tq, V7X_LANES), F32), pltpu.VMEM((2 * tq, 2 * D_HEAD), F32)],
        compiler_params=_params("parallel", "parallel", "arbitrary"),
        name="attn_prompt",
    )(lam.reshape(1, 1), q, k, v, subw.reshape(1, D_HEAD))


def _attn_sample_kernel(pt_ref, lam_ref, q_ref, kn_ref, vn_ref, subw_ref, *refs, lam_init, t_new):
    k_refs = refs[:PAGES_PER_STEP]
    v_refs = refs[PAGES_PER_STEP:2 * PAGES_PER_STEP]
    o_ref, qm_scr, bias_scr, m_scr, l_scr, acc_scr = refs[2 * PAGES_PER_STEP:]
    g = pl.program_id(1)
    n_rows = 2 * H_ATT * t_new
    n_cols = PAGE_SIZE * H_ATT
    row_head = lax.broadcasted_iota(jnp.int32, (n_rows, 1), 0) // (2 * t_new)
    lane = lax.broadcasted_iota(jnp.int32, (1, D_HEAD), 1)

    @pl.when(g == 0)
    def _():
        q = q_ref[0] * (HD_ATT ** -0.5)
        pieces = []
        for h in range(H_ATT):
            qh = q[:, h * D_HEAD:(h + 1) * D_HEAD]
            pieces += [jnp.where(lane < HD_ATT, qh, 0.0), jnp.where(lane < HD_ATT, 0.0, qh)]
        qm_scr[...] = jnp.concatenate(pieces, axis=0).astype(BF16)
        col_head = lax.broadcasted_iota(jnp.int32, (1, n_cols), 1) % H_ATT
        bias_scr[...] = jnp.where(row_head == col_head, 0.0, NEG_INF)
        m_scr[...] = jnp.full(m_scr.shape, NEG_INF, F32)
        l_scr[...] = jnp.zeros(l_scr.shape, F32)
        acc_scr[...] = jnp.zeros(acc_scr.shape, F32)

    def update(scores, values):
        m, l, acc = m_scr[...], l_scr[...], acc_scr[...]
        for s, vmat in zip(scores, values):
            m_new = jnp.maximum(m, jnp.max(s, axis=-1, keepdims=True))
            alpha = jnp.exp(m - m_new)
            p = jnp.concatenate([jnp.exp(s[:, c * V7X_LANES:(c + 1) * V7X_LANES] - m_new)
                                 for c in range(s.shape[1] // V7X_LANES)], axis=1)
            l = alpha * l + jnp.sum(p, axis=-1, keepdims=True)
            acc = alpha * acc + jnp.dot(p.astype(BF16), vmat, preferred_element_type=F32)
            m = m_new
        m_scr[...], l_scr[...], acc_scr[...] = m, l, acc

    qm = qm_scr[...]
    keys = [k_refs[i][0, 0].reshape(n_cols, D_HEAD).astype(BF16) for i in range(PAGES_PER_STEP)]
    values = [v_refs[i][0, 0].reshape(n_cols, D_HEAD).astype(BF16) for i in range(PAGES_PER_STEP)]
    update([lax.dot_general(qm, kf, _NT, preferred_element_type=F32) + bias_scr[...] for kf in keys], values)

    @pl.when(g == pl.num_programs(1) - 1)
    def _():
        n_new = t_new * H_ATT
        pad = jnp.zeros((V7X_LANES - n_new, D_HEAD), BF16)
        kn = jnp.concatenate([kn_ref[0].reshape(n_new, D_HEAD).astype(BF16), pad], axis=0)
        vn = jnp.concatenate([vn_ref[0].reshape(n_new, D_HEAD).astype(BF16), pad], axis=0)
        col = lax.broadcasted_iota(jnp.int32, (1, V7X_LANES), 1)
        qpos = lax.broadcasted_iota(jnp.int32, (n_rows, 1), 0) % t_new
        ok = (col < n_new) & (col % H_ATT == row_head) & (col // H_ATT <= qpos)
        update([jnp.where(ok, lax.dot_general(qm, kn, _NT, preferred_element_type=F32), NEG_INF)], [vn])
        acc = acc_scr[...]
        l = l_scr[...]
        lam = lam_ref[0, 0]
        for h in range(H_ATT):
            rows = slice(2 * h * t_new, 2 * (h + 1) * t_new)
            o_ref[0, :, h * D_HEAD:(h + 1) * D_HEAD] = _sub_norm(acc[rows], l[rows], lam, subw_ref[...], lam_init)


def _attn_sample(q, k_new, v_new, cache_k, cache_v, layer_j, page_table, lam, subw, lam_init):
    n_batch, t_new, _ = q.shape
    n_pages = page_table.shape[1]
    assert n_pages % PAGES_PER_STEP == 0 and t_new <= PAGE_SIZE and t_new % V7X_SUBLANES == 0
    n_steps = n_pages // PAGES_PER_STEP
    n_rows = 2 * H_ATT * t_new
    assert n_rows == V7X_LANES and t_new * H_ATT <= V7X_LANES

    def page_spec(i):
        return pl.BlockSpec((1, 1, PAGE_SIZE, H_ATT, D_HEAD),
                            lambda b, g, pt: (layer_j, pt[b, g * PAGES_PER_STEP + i], 0, 0, 0))

    tok_spec = pl.BlockSpec((1, t_new, D_MODEL), lambda b, g, pt: (b, 0, 0))
    new_spec = pl.BlockSpec((1, t_new, H_ATT, D_HEAD), lambda b, g, pt: (b, 0, 0, 0))
    grid_spec = pltpu.PrefetchScalarGridSpec(
        num_scalar_prefetch=1,
        grid=(n_batch, n_steps),
        in_specs=[pl.BlockSpec(memory_space=pltpu.SMEM), tok_spec, new_spec, new_spec,
                  pl.BlockSpec((1, D_HEAD), lambda b, g, pt: (0, 0))]
                 + [page_spec(i) for i in range(PAGES_PER_STEP)] * 2,
        out_specs=tok_spec,
        scratch_shapes=[pltpu.VMEM((n_rows, D_HEAD), BF16),
                        pltpu.VMEM((n_rows, PAGE_SIZE * H_ATT), F32),
                        pltpu.VMEM((n_rows, V7X_LANES), F32), pltpu.VMEM((n_rows, V7X_LANES), F32),
                        pltpu.VMEM((n_rows, D_HEAD), F32)])
    new_shape = (n_batch, t_new, H_ATT, D_HEAD)
    return pl.pallas_call(
        functools.partial(_attn_sample_kernel, lam_init=lam_init, t_new=t_new),
        grid_spec=grid_spec,
        out_shape=jax.ShapeDtypeStruct((n_batch, t_new, D_MODEL), F32),
        compiler_params=_params("parallel", "arbitrary"),
        name="attn_sample",
    )(page_table, lam.reshape(1, 1), q, k_new.reshape(new_shape), v_new.reshape(new_shape), subw.reshape(1, D_HEAD),
      *([cache_k] * PAGES_PER_STEP), *([cache_v] * PAGES_PER_STEP))


ROUTE_LANES = ("expert0", "expert1", "gate0", "gate1", "rank0", "rank1")


def _router_kernel(h_ref, w_ref, b_ref, o_ref, cnt_ref, carry_scr):
    i = pl.program_id(0)
    tm = h_ref.shape[0]

    @pl.when(i == 0)
    def _():
        carry_scr[...] = jnp.zeros(carry_scr.shape, F32)

    logits = _mm(h_ref[...], w_ref[...], precise=True) + b_ref[...]
    lane = lax.broadcasted_iota(jnp.int32, (1, V7X_LANES), 1)
    lane_f = lane.astype(F32)

    def max_first(mask):
        v = jnp.where(mask, logits, -jnp.inf)
        m = jnp.max(v, axis=-1, keepdims=True)
        return m, jnp.min(jnp.where(v == m, lane_f, float(V7X_LANES)), axis=-1, keepdims=True)

    is_group = lane < N_GROUPS
    m_g, grp = max_first(is_group)
    g_gate = 1.0 / jnp.sum(jnp.where(is_group, jnp.exp(logits - m_g), 0.0), axis=-1, keepdims=True)
    first = N_GROUPS + EXPERTS_PER_GROUP * grp
    in_group = (lane_f >= first) & (lane_f < first + EXPERTS_PER_GROUP)
    v1, l1 = max_first(in_group)
    v2, l2 = max_first(in_group & (lane_f != l1))
    e21 = jnp.exp(v2 - v1)
    gate0 = g_gate / (1.0 + e21)
    gate1 = g_gate * e21 / (1.0 + e21)

    hot0 = lane_f == l1
    hot1 = lane_f == l2
    cnt = hot0.astype(F32) + hot1.astype(F32)
    earlier = lax.broadcasted_iota(jnp.int32, (tm, tm), 0) > lax.broadcasted_iota(jnp.int32, (tm, tm), 1)
    before = carry_scr[...] + jnp.dot(earlier.astype(BF16), cnt.astype(BF16), preferred_element_type=F32)
    rank0 = jnp.sum(jnp.where(hot0, before, 0.0), axis=-1, keepdims=True)
    rank1 = jnp.sum(jnp.where(hot1, before, 0.0), axis=-1, keepdims=True)
    carry_scr[...] = carry_scr[...] + jnp.sum(cnt, axis=0, keepdims=True)
    cnt_ref[...] = carry_scr[...]

    cols = dict(expert0=l1 - N_GROUPS, expert1=l2 - N_GROUPS, gate0=gate0, gate1=gate1, rank0=rank0, rank1=rank1)
    out = jnp.zeros((tm, V7X_LANES), F32)
    for k, name in enumerate(ROUTE_LANES):
        out = jnp.where(lane == k, cols[name], out)
    o_ref[...] = out


def _route(h, w_pad, b_pad):
    n, d = h.shape
    tm = min(512, n)
    assert n % tm == 0
    return pl.pallas_call(
        _router_kernel,
        grid=(n // tm,),
        in_specs=[pl.BlockSpec((tm, d), lambda i: (i, 0)),
                  pl.BlockSpec(w_pad.shape, lambda i: (0, 0)),
                  pl.BlockSpec((1, V7X_LANES), lambda i: (0, 0))],
        out_specs=[pl.BlockSpec((tm, V7X_LANES), lambda i: (i, 0)), pl.BlockSpec((1, V7X_LANES), lambda i: (0, 0))],
        out_shape=[jax.ShapeDtypeStruct((n, V7X_LANES), F32), jax.ShapeDtypeStruct((1, V7X_LANES), F32)],
        scratch_shapes=[pltpu.VMEM((1, V7X_LANES), F32)],
        compiler_params=_params("arbitrary"),
        name="route",
    )(h, w_pad, b_pad)


def _row_copy(src, src_row, dst, dst_row, sem):
    return pltpu.make_async_copy(src.at[pl.ds(src_row, 1), :], dst.at[pl.ds(dst_row, 1), :], sem)


def _scatter_rows_kernel(dest_ref, h_ref, xg_init_ref, xg_ref, sem):
    del xg_init_ref
    n_copies = TOP_K * h_ref.shape[0]

    def start(a, carry):
        _row_copy(h_ref, lax.shift_right_logical(a, 1), xg_ref, dest_ref[0, 0, a], sem).start()
        return carry

    def wait(a, carry):
        _row_copy(h_ref, 0, xg_ref, 0, sem).wait()
        return carry

    lax.fori_loop(0, n_copies, start, 0, unroll=8)
    lax.fori_loop(0, n_copies, wait, 0, unroll=8)


def _scatter_rows(h, dest, n_rows):
    n, d = h.shape
    tm = min(256, n)
    assert n % tm == 0 and TOP_K == 2
    return pl.pallas_call(
        _scatter_rows_kernel,
        grid=(n // tm,),
        in_specs=[pl.BlockSpec((1, 1, TOP_K * tm), lambda i: (i, 0, 0), memory_space=pltpu.SMEM),
                  pl.BlockSpec((tm, d), lambda i: (i, 0)),
                  pl.BlockSpec(memory_space=pl.ANY)],
        out_specs=pl.BlockSpec(memory_space=pl.ANY),
        out_shape=jax.ShapeDtypeStruct((n_rows, d), F32),
        scratch_shapes=[pltpu.SemaphoreType.DMA(())],
        input_output_aliases={2: 0},
        compiler_params=_params("arbitrary"),
        name="scatter_rows",
    )(dest.reshape(n // tm, 1, TOP_K * tm), h, jnp.zeros((n_rows, d), F32))


def _combine_kernel(dest_ref, dest_next_ref, x_ref, gates_ref, g_ref, y_hbm_ref, o_ref, ybuf, sem, *, normed):
    i = pl.program_id(0)
    tm = x_ref.shape[0]
    slot = i % 2

    def fetch(d_ref, s):
        def start(a, carry):
            _row_copy(y_hbm_ref, d_ref[0, 0, a], ybuf.at[s, a & 1], lax.shift_right_logical(a, 1), sem.at[s]).start()
            return carry
        lax.fori_loop(0, TOP_K * tm, start, 0, unroll=8)

    @pl.when(i == 0)
    def _():
        fetch(dest_ref, 0)

    @pl.when(i + 1 < pl.num_programs(0))
    def _():
        fetch(dest_next_ref, 1 - slot)

    def wait(a, carry):
        _row_copy(y_hbm_ref, 0, ybuf.at[slot, 0], 0, sem.at[slot]).wait()
        return carry

    lax.fori_loop(0, TOP_K * tm, wait, 0, unroll=8)
    gates = gates_ref[...]
    x = x_ref[...] + (ybuf[slot, 0] * gates[:, 0:1] + ybuf[slot, 1] * gates[:, 1:2])
    o_ref[...] = _rms(x, g_ref[...]) if normed else x


def _combine(x, y_rows, dest, gates, g, normed):
    n, d = x.shape
    tm = min(256, n)
    assert n % tm == 0 and TOP_K == 2
    n_tiles = n // tm
    row = pl.BlockSpec((tm, d), lambda i: (i, 0))
    dest_tiles = dest.reshape(n_tiles, 1, TOP_K * tm)
    return pl.pallas_call(
        functools.partial(_combine_kernel, normed=normed),
        grid=(n_tiles,),
        in_specs=[pl.BlockSpec((1, 1, TOP_K * tm), lambda i: (i, 0, 0), memory_space=pltpu.SMEM),
                  pl.BlockSpec((1, 1, TOP_K * tm), lambda i: (jnp.minimum(i + 1, n_tiles - 1), 0, 0),
                               memory_space=pltpu.SMEM),
                  row, pl.BlockSpec((tm, TOP_K), lambda i: (i, 0)), pl.BlockSpec((1, d), lambda i: (0, 0)),
                  pl.BlockSpec(memory_space=pl.ANY)],
        out_specs=row,
        out_shape=jax.ShapeDtypeStruct((n, d), F32),
        scratch_shapes=[pltpu.VMEM((2, TOP_K, tm, d), F32), pltpu.SemaphoreType.DMA((2,))],
        compiler_params=_params("arbitrary"),
        name="combine",
    )(dest_tiles, dest_tiles, x, gates, g.reshape(1, d), y_rows)


def _expert_kernel(be_ref, bv_ref, x_ref, wg_ref, wu_ref, wd_ref, y_ref, wg_scr, wu_scr, wd_scr):
    i = pl.program_id(0)

    @pl.when((i == 0) | (be_ref[i] != be_ref[jnp.maximum(i - 1, 0)]))
    def _():
        wg_scr[...] = wg_ref[0, 0].astype(BF16)
        wu_scr[...] = wu_ref[0, 0].astype(BF16)
        wd_scr[...] = wd_ref[0, 0].astype(BF16)

    @pl.when(bv_ref[i] > 0)
    def _():
        x = x_ref[...].astype(BF16)
        hg = jnp.dot(x, wg_scr[...], preferred_element_type=F32)
        hu = jnp.dot(x, wu_scr[...], preferred_element_type=F32)
        hdn = (hg * _sigmoid(hg)) * hu
        y_ref[...] = jnp.dot(hdn.astype(BF16), wd_scr[...], preferred_element_type=F32)

    @pl.when(bv_ref[i] == 0)
    def _():
        y_ref[...] = jnp.zeros(y_ref.shape, F32)


def _expert_blocks(xg, block_expert, block_valid, wg, wu, wd, layer):
    n_rows, d = xg.shape
    n_blocks = n_rows // MOE_ROWS
    grid_spec = pltpu.PrefetchScalarGridSpec(
        num_scalar_prefetch=2,
        grid=(n_blocks,),
        in_specs=[pl.BlockSpec((MOE_ROWS, d), lambda i, be, bv: (i, 0)),
                  pl.BlockSpec((1, 1, d, D_EXPERT), lambda i, be, bv: (layer, be[i], 0, 0)),
                  pl.BlockSpec((1, 1, d, D_EXPERT), lambda i, be, bv: (layer, be[i], 0, 0)),
                  pl.BlockSpec((1, 1, D_EXPERT, d), lambda i, be, bv: (layer, be[i], 0, 0))],
        out_specs=pl.BlockSpec((MOE_ROWS, d), lambda i, be, bv: (i, 0)),
        scratch_shapes=[pltpu.VMEM((d, D_EXPERT), BF16), pltpu.VMEM((d, D_EXPERT), BF16),
                        pltpu.VMEM((D_EXPERT, d), BF16)])
    return pl.pallas_call(
        _expert_kernel,
        grid_spec=grid_spec,
        out_shape=jax.ShapeDtypeStruct((n_rows, d), F32),
        compiler_params=_params("arbitrary"),
        name="expert_blocks",
    )(block_expert, block_valid, xg, wg, wu, wd)


def _hier_moe(h, moe):
    n, d = h.shape
    routed, lane_counts = _route(h, moe["w_router"], moe["b_router"])
    col = {name: routed[:, k] for k, name in enumerate(ROUTE_LANES)}
    eidx = jnp.stack([col["expert0"], col["expert1"]], axis=1).astype(jnp.int32)
    rank = jnp.stack([col["rank0"], col["rank1"]], axis=1).astype(jnp.int32)
    gates = jnp.stack([col["gate0"], col["gate1"]], axis=1)
    counts = lane_counts[0, N_GROUPS:N_GROUPS + N_EXPERTS].astype(jnp.int32)

    padded = (counts + MOE_ROWS - 1) // MOE_ROWS * MOE_ROWS
    pad_end = jnp.cumsum(padded)
    pad_start = pad_end - padded
    experts = jnp.arange(N_EXPERTS, dtype=jnp.int32)
    dest = jnp.sum(jnp.where(eidx[:, :, None] == experts, pad_start, 0), axis=-1) + rank
    n_blocks = -(-n * TOP_K // MOE_ROWS) + N_EXPERTS
    block_start = jnp.arange(n_blocks, dtype=jnp.int32) * MOE_ROWS
    block_expert = jnp.sum((block_start[:, None] >= pad_end[None, :]).astype(jnp.int32), axis=1)
    block_expert = jnp.minimum(block_expert, N_EXPERTS - 1)
    block_valid = (block_start < pad_end[-1]).astype(jnp.int32)
    xg = _scatter_rows(h, dest, n_blocks * MOE_ROWS)
    y_rows = _expert_blocks(xg, block_expert, block_valid, moe["wg"], moe["wu"], moe["wd"], moe["layer"])
    return y_rows, dest, gates


def _mix_params(j, precise, w_in_mix, mu_shift, w_decay0, w_decay_up, w_iclr0, w_iclr_up, w_gate_up,
                k_k, k_a, r_k, ln_x_w, ln_x_b, pool_w, pool_scale, w_out_mix):
    wdt = F32 if precise else BF16
    zero = jnp.zeros((DECAY_LORA, D_RWKV), F32)
    vec = lambda a: a[j].reshape(1, -1)
    return dict(
        w_in=w_in_mix[j].astype(wdt), mu=vec(mu_shift), w0=vec(w_decay0),
        wcomb=jnp.block([[w_decay_up[j], zero], [zero, w_iclr_up[j]]]).astype(wdt),
        a0=vec(w_iclr0), wgate=w_gate_up[j].astype(wdt), k_k=vec(k_k), k_a=vec(k_a), r_k=vec(r_k),
        ln_w=vec(ln_x_w), ln_b=vec(ln_x_b), pool_w=pool_w[j].astype(wdt), pool_scale=pool_scale[j],
        w_out_a=w_out_mix[j, :D_RWKV].astype(wdt), w_out_b=w_out_mix[j, D_RWKV:].astype(wdt))


def _mixer_group(x, n_seq, t_len, pos0, shift0, wkv0, pool0, mix, norm_g, norm_ffn_g, precise):
    zr, u = _norm_matmul(x, norm_g, mix["w_in"], (N_SHIFT, D_POOL), precise)
    zr = zr.reshape(n_seq, t_len, N_SHIFT)
    t_pad = -(-t_len // WKV_CHUNK) * WKV_CHUNK
    zr_pad = jnp.pad(zr, ((0, 0), (0, t_pad - t_len), (0, 0))) if t_pad != t_len else zr
    y_rwkv, s_bd = _rwkv_mix(zr_pad, n_seq, t_pad, min(t_len, WKV_CHUNK), shift0, _state_to_block_diag(wkv0),
                             mix, precise)
    y_rwkv = y_rwkv[:, :t_len].reshape(n_seq * t_len, D_RWKV)
    hist = jnp.pad(pool0, ((0, 0), (POOL_CARRY - POOL_BUF, 0), (0, 0)))
    y_pool = _pool_mix(u, n_seq, t_len, hist, pos0, mix["pool_w"], mix["pool_scale"], precise)
    x, h = _matmul_res(x, [(y_rwkv, mix["w_out_a"]), (y_pool, mix["w_out_b"])], norm_ffn_g, precise)
    ext = jnp.concatenate([pool0, u.reshape(n_seq, t_len, D_POOL)], axis=1)
    return x, h, zr[:, -1], _state_from_block_diag(s_bd), ext[:, -POOL_BUF:]


def kernel(x_prompt, x_sample, state_wkv, state_shift, state_pool, cache_k, cache_v, page_table,
           norm_mix, norm_ffn, norm_final,
           w_in_mix, mu_shift, w_decay0, w_decay_up, w_iclr0, w_iclr_up, w_gate_up,
           k_k, k_a, r_k, ln_x_w, ln_x_b, pool_w, pool_scale, w_out_mix,
           w_qkv, lam_q1, lam_k1, lam_q2, lam_k2, subln_w, w_o,
           router_group, router_group_bias, router_expert, router_expert_bias,
           w_exp_gate, w_exp_up, w_exp_down):
    bp, tp, d = x_prompt.shape
    bs, ts, _ = x_sample.shape
    depth = norm_mix.shape[0]
    past_len = page_table.shape[1] * PAGE_SIZE
    groups = [dict(x=x_prompt.reshape(bp * tp, d), n_seq=bp, t=tp, precise=False),
              dict(x=x_sample.reshape(bs * ts, d), n_seq=bs, t=ts, precise=True)]
    outs = [dict(wkv=[], shift=[], pool=[], k=[], v=[]) for _ in groups]

    for layer in range(depth):
        j = layer // 2
        if layer % 2 == 0:
            mix_w = (w_in_mix, mu_shift, w_decay0, w_decay_up, w_iclr0, w_iclr_up, w_gate_up,
                     k_k, k_a, r_k, ln_x_w, ln_x_b, pool_w, pool_scale, w_out_mix)
            inits = [(0, jnp.zeros((bp, N_SHIFT), F32), jnp.zeros((bp, H_RWKV, HD_RWKV, HD_RWKV), F32),
                      jnp.zeros((bp, POOL_BUF, D_POOL), F32)),
                     (past_len, state_shift[j], state_wkv[j], state_pool[j])]
            for grp, out, (pos0, sh0, wk0, pl0) in zip(groups, outs, inits):
                mix = _mix_params(j, grp["precise"], *mix_w)
                grp["x"], grp["h"], sh, wk, po = _mixer_group(grp["x"], grp["n_seq"], grp["t"], pos0, sh0, wk0, pl0,
                                                              mix, norm_mix[layer], norm_ffn[layer], grp["precise"])
                out["wkv"].append(wk); out["shift"].append(sh); out["pool"].append(po)
        else:
            lam_init = 0.8 - 0.6 * math.exp(-0.3 * layer)
            lam = (jnp.exp(jnp.sum(lam_q1[j] * lam_k1[j])) - jnp.exp(jnp.sum(lam_q2[j] * lam_k2[j])) + lam_init)
            for gi, (grp, out) in enumerate(zip(groups, outs)):
                n_seq, t, precise = grp["n_seq"], grp["t"], grp["precise"]
                wdt = F32 if precise else BF16
                q, k, v = _norm_matmul(grp["x"], norm_mix[layer], w_qkv[j].astype(wdt), (d, d, d), precise)
                q3, k3, v3 = (a.reshape(n_seq, t, d) for a in (q, k, v))
                if gi == 0:
                    o = _attn_prompt(q3, k3, v3, lam, subln_w[j], lam_init, n_seq, t)
                else:
                    o = _attn_sample(q3, k3, v3, cache_k, cache_v, j, page_table, lam, subln_w[j], lam_init)
                grp["x"], grp["h"] = _matmul_res(grp["x"], [(o.reshape(n_seq * t, d), w_o[j].astype(wdt))],
                                                 norm_ffn[layer], precise)
                out["k"].append(k.reshape(n_seq, t, H_ATT, D_HEAD)); out["v"].append(v.reshape(n_seq, t, H_ATT, D_HEAD))

        n_router = N_GROUPS + N_EXPERTS
        moe = dict(
            w_router=jnp.pad(jnp.concatenate([router_group[layer], router_expert[layer]], axis=1),
                             ((0, 0), (0, V7X_LANES - n_router))),
            b_router=jnp.pad(jnp.concatenate([router_group_bias[layer], router_expert_bias[layer]]),
                             (0, V7X_LANES - n_router)).reshape(1, V7X_LANES),
            wg=w_exp_gate, wu=w_exp_up, wd=w_exp_down, layer=layer)
        for grp in groups:
            y_rows, dest, gates = _hier_moe(grp["h"], moe)
            grp["x"] = _combine(grp["x"], y_rows, dest, gates, norm_final, normed=(layer == depth - 1))

    res = []
    for grp, out in zip(groups, outs):
        res.append([grp["x"].reshape(grp["n_seq"], grp["t"], d), jnp.stack(out["wkv"]), jnp.stack(out["shift"]),
                    jnp.stack(out["pool"]), jnp.stack(out["k"]), jnp.stack(out["v"])])
    (yp, wkp, shp, pop, kp, vp), (ys, wks, shs, pos_, ks_, vs_) = res
    return (yp, ys, wkp, shp, pop, kp, vp, wks, shs, pos_, ks_, vs_)
```

```python
import functools
import math

import jax
import jax.numpy as jnp
from jax import lax
from jax.experimental import pallas as pl
from jax.experimental.pallas import tpu as pltpu

F32 = jnp.float32
BF16 = jnp.bfloat16

V7X_LANES = 128
V7X_SUBLANES = 8
V7X_VMEM_LIMIT_BYTES = 56 * 1024 * 1024

D_MODEL = 1024
RMS_EPS = 1e-5
NEG_INF = -1e30
D_RWKV = 512
HD_RWKV = 64
H_RWKV = 8
N_PAIRS = H_RWKV // 2
DECAY_LORA = 64
ICLR_LORA = 64
GATE_LORA = 128
LN_X_EPS = 64e-5
D_POOL = 512
POOL_WINDOWS = (2, 4, 8, 16)
POOL_GROUP = 128
POOL_BUF = 15
POOL_CARRY = 16
N_SHIFT = 3 * D_RWKV + GATE_LORA + DECAY_LORA + ICLR_LORA
HD_ATT = 64
H_ATT = 8
D_HEAD = 2 * HD_ATT
PAGE_SIZE = 128
N_GROUPS = 4
EXPERTS_PER_GROUP = 8
N_EXPERTS = 32
TOP_K = 2
D_EXPERT = 512
WKV_CHUNK = 64
WKV_SEQS = 4
MOE_ROWS = 256
PAGES_PER_STEP = 16
ATTN_STRIP = 128
ATTN_BLOCKS = 4


def _params(*semantics):
    return pltpu.CompilerParams(dimension_semantics=semantics, vmem_limit_bytes=V7X_VMEM_LIMIT_BYTES)


_NN = (((1,), (0,)), ((), ()))
_NT = (((1,), (1,)), ((), ()))
_TN = (((0,), (0,)), ((), ()))


def _hi_lo(x):
    hi = x.astype(BF16)
    return hi, (None if x.dtype == BF16 else (x - hi.astype(F32)).astype(BF16))


def _mm(a, b, dims=_NN, precise=False):
    dot = functools.partial(lax.dot_general, dimension_numbers=dims, preferred_element_type=F32)
    if not precise:
        return dot(a.astype(BF16), b.astype(BF16))
    a_hi, a_lo = _hi_lo(a)
    b_hi, b_lo = _hi_lo(b)
    out = dot(a_hi, b_hi)
    if b_lo is not None:
        out = out + dot(a_hi, b_lo)
    if a_lo is not None:
        out = out + dot(a_lo, b_hi)
    return out


def _sigmoid(x):
    return 1.0 / (1.0 + jnp.exp(-x))


def _softplus(x):
    return jnp.maximum(x, 0.0) + jnp.log(1.0 + jnp.exp(-jnp.abs(x)))


def _rms(x, g):
    return x * lax.rsqrt(jnp.mean(x * x, axis=-1, keepdims=True) + RMS_EPS) * g


def _norm_matmul_kernel(x_ref, g_ref, w_ref, *out_refs, precise):
    h = _rms(x_ref[...], g_ref[...])
    off = 0
    for o_ref in out_refs:
        n = o_ref.shape[-1]
        o_ref[...] = _mm(h, w_ref[:, off:off + n], precise=precise)
        off += n


def _norm_matmul(x, g, w, splits, precise):
    n, d = x.shape
    tm = min(512, n)
    assert n % tm == 0 and sum(splits) == w.shape[1]
    return pl.pallas_call(
        functools.partial(_norm_matmul_kernel, precise=precise),
        grid=(n // tm,),
        in_specs=[pl.BlockSpec((tm, d), lambda i: (i, 0)),
                  pl.BlockSpec((1, d), lambda i: (0, 0)),
                  pl.BlockSpec(w.shape, lambda i: (0, 0))],
        out_specs=[pl.BlockSpec((tm, s), lambda i: (i, 0)) for s in splits],
        out_shape=[jax.ShapeDtypeStruct((n, s), F32) for s in splits],
        compiler_params=_params("parallel"),
        name="norm_matmul",
    )(x, g.reshape(1, d), w)


def _matmul_res_kernel(res_ref, g_ref, *refs, precise):
    xo_ref, ho_ref = refs[-2:]
    acc = res_ref[...]
    for a_ref, w_ref in zip(refs[0:-2:2], refs[1:-2:2]):
        acc = acc + _mm(a_ref[...], w_ref[...], precise=precise)
    xo_ref[...] = acc
    ho_ref[...] = _rms(acc, g_ref[...])


def _matmul_res(res, pairs, g, precise):
    n, d = res.shape
    tm = min(512, n)
    assert n % tm == 0
    in_specs = [pl.BlockSpec((tm, d), lambda i: (i, 0)), pl.BlockSpec((1, d), lambda i: (0, 0))]
    args = [res, g.reshape(1, d)]
    for a, w in pairs:
        in_specs += [pl.BlockSpec((tm, a.shape[1]), lambda i: (i, 0)), pl.BlockSpec(w.shape, lambda i: (0, 0))]
        args += [a, w]
    return pl.pallas_call(
        functools.partial(_matmul_res_kernel, precise=precise),
        grid=(n // tm,),
        in_specs=in_specs,
        out_specs=[pl.BlockSpec((tm, d), lambda i: (i, 0))] * 2,
        out_shape=[jax.ShapeDtypeStruct((n, d), F32)] * 2,
        compiler_params=_params("parallel"),
        name="matmul_res",
    )(*args)


def _block_diag(x, left):
    return jnp.concatenate([jnp.where(left, x, 0.0), jnp.where(left, 0.0, x)], axis=0)


def _rwkv_kernel(zr_ref, shift0_ref, s0_ref, mu_ref, w0_ref, wcomb_ref, a0_ref, wgate_ref, kk_ref, ka_ref,
                 rk_ref, lnw_ref, lnb_ref, e_ref, y_ref, sout_ref, s_scr, prev_scr, *, t_valid, precise):
    c_len = WKV_CHUNK
    n_sb = zr_ref.shape[0]
    c = pl.program_id(1)
    mm = functools.partial(_mm, precise=precise)

    @pl.when(c == 0)
    def _():
        s_scr[...] = s0_ref[...]
        prev_scr[...] = shift0_ref[...]

    row = lax.broadcasted_iota(jnp.int32, (c_len, 1), 0)
    lane = lax.broadcasted_iota(jnp.int32, (1, V7X_LANES), 1)
    left = lane < HD_RWKV
    valid = row < t_valid
    s_le_t = (lane % HD_RWKV) <= row
    s_lt_t = (lane % HD_RWKV) < row
    same_head = (lax.broadcasted_iota(jnp.int32, (V7X_LANES, 1), 0) < HD_RWKV) == left
    n_doublings = int(math.log2(c_len))
    e = e_ref[...]

    def seg_sum(x):
        return _mm(x, e, precise=True)

    def cat(x, y):
        return jnp.concatenate([x, y], axis=0)

    def bd(x):
        return _block_diag(x, left)

    seqs = range(n_sb)
    zr = [zr_ref[sq] for sq in seqs]
    z_prev = [jnp.where(row == 0, prev_scr[sq], pltpu.roll(zr[sq], 1, 0)) for sq in seqs]
    for sq in seqs:
        prev_scr[sq] = zr[sq][c_len - 1:c_len, :]
    zs = [zr[sq] + (z_prev[sq] - zr[sq]) * mu_ref[...] for sq in seqs]
    o3 = 3 * D_RWKV
    da = [zs[sq][:, o3 + GATE_LORA:] for sq in seqs]
    pre = [mm(jnp.where(left, jnp.tanh(x), x), wcomb_ref[...]) for x in da]
    gate = [mm(_sigmoid(zs[sq][:, o3:o3 + GATE_LORA]), wgate_ref[...]) for sq in seqs]

    chains = [(sq, p) for sq in seqs for p in range(N_PAIRS)]
    idx = range(len(chains))
    sls = [slice(p * V7X_LANES, (p + 1) * V7X_LANES) for _, p in chains]

    def cols(sq, p, group):
        lo = group * D_RWKV + p * V7X_LANES
        return zs[sq][:, lo:lo + V7X_LANES]

    r = [cols(sq, p, 0) for sq, p in chains]
    k = [cols(sq, p, 1) for sq, p in chains]
    v = [cols(sq, p, 2) for sq, p in chains]
    lw = [-jnp.exp(-_softplus(-(w0_ref[:, sls[i]] + pre[sq][:, sls[i]])) - 0.5)
          for i, (sq, p) in enumerate(chains)]
    iclr = [_sigmoid(a0_ref[:, sls[i]] + pre[sq][:, D_RWKV + p * V7X_LANES:D_RWKV + (p + 1) * V7X_LANES])
            for i, (sq, p) in enumerate(chains)]
    kk = [k[i] * kk_ref[:, sls[i]] for i in idx]
    kk_ss = [seg_sum(kk[i] * kk[i]) for i in idx]
    kk = [kk[i] * lax.rsqrt(jnp.maximum(kk_ss[i], 1e-24)) for i in idx]
    k = [k[i] * (1.0 + (iclr[i] - 1.0) * ka_ref[:, sls[i]]) for i in idx]
    a = [-kk[i] for i in idx]
    b = [kk[i] * iclr[i] for i in idx]
    if t_valid < c_len:
        lw, a, b, k, v = ([jnp.where(valid, x, 0.0) for x in xs] for xs in (lw, a, b, k, v))

    def cumsum_rows(x):
        sh = 1
        while sh < c_len:
            x = x + jnp.where(row >= sh, pltpu.roll(x, sh, 0), 0.0)
            sh *= 2
        return x

    g = [cumsum_rows(x) for x in lw]
    e_g = [jnp.exp(x) for x in g]
    e_ng = [jnp.exp(-x) for x in g]
    g_end = [x[c_len - 1:c_len, :] for x in e_g]
    a_t = [a[i] * jnp.exp(g[i] - lw[i]) for i in idx]
    b_t = [b[i] * e_ng[i] for i in idx]
    k_t = [k[i] * e_ng[i] for i in idx]
    r_t = [r[i] * e_g[i] for i in idx]

    s = [s_scr[sq, p] for sq, p in chains]
    bd_v = [bd(x) for x in v]
    sc = [mm(cat(a_t[i], r_t[i]), cat(bd(b_t[i]), bd(k_t[i])), _NT) for i in idx]
    a_ab = [jnp.where(s_lt_t, x[:c_len, :V7X_LANES], 0.0) for x in sc]
    a_ak = [jnp.where(s_lt_t, x[:c_len, V7X_LANES:], 0.0) for x in sc]
    a_rb = [jnp.where(s_le_t, x[c_len:, :V7X_LANES], 0.0) for x in sc]
    a_rk = [jnp.where(s_le_t, x[c_len:, V7X_LANES:], 0.0) for x in sc]

    x = [mm(a_t[i], s[i], _NT) + mm(a_ak[i], bd_v[i]) for i in idx]
    lmat = a_ab
    for it in range(n_doublings):
        x = [x[i] + mm(lmat[i], bd(x[i])) for i in idx]
        if it + 1 < n_doublings:
            lmat = [mm(lmat[i], bd(lmat[i])) for i in idx]
    u = x
    y = [mm(r_t[i], s[i], _NT) + mm(a_rb[i], bd(u[i])) + mm(a_rk[i], bd_v[i]) for i in idx]
    s_new = [s[i] * g_end[i] + mm(cat(u[i], v[i]), cat(b_t[i] * g_end[i], k_t[i] * g_end[i]), _TN) for i in idx]
    for i, (sq, p) in enumerate(chains):
        s_scr[sq, p] = jnp.where(same_head, s_new[i], 0.0)

    mean = [seg_sum(x) * (1.0 / HD_RWKV) for x in y]
    d = [y[i] - mean[i] for i in idx]
    var = [seg_sum(x * x) * (1.0 / HD_RWKV) for x in d]
    bonus_rk = [seg_sum(r[i] * k[i] * rk_ref[:, sls[i]]) for i in idx]
    for i, (sq, p) in enumerate(chains):
        yn = d[i] * lax.rsqrt(var[i] + LN_X_EPS) * lnw_ref[:, sls[i]] + lnb_ref[:, sls[i]]
        y_ref[sq, :, sls[i]] = (yn + bonus_rk[i] * v[i]) * gate[sq][:, sls[i]]

    sout_ref[...] = s_scr[...]


def _rwkv_mix(zr, n_seq, t_pad, t_valid, shift0, s0_bd, mix, precise):
    n_chunks = t_pad // WKV_CHUNK
    n_sb = math.gcd(n_seq, WKV_SEQS)
    assert t_pad % WKV_CHUNK == 0 and (t_valid == WKV_CHUNK or n_chunks == 1)
    rowvec = lambda n: pl.BlockSpec((1, n), lambda s, c: (0, 0))
    full = lambda a: pl.BlockSpec(a.shape, lambda s, c: (0,) * a.ndim)
    state_spec = pl.BlockSpec((n_sb, N_PAIRS, V7X_LANES, V7X_LANES), lambda s, c: (s, 0, 0, 0))
    e128 = jnp.kron(jnp.eye(2, dtype=F32), jnp.ones((HD_RWKV, HD_RWKV), F32)).astype(BF16)
    return pl.pallas_call(
        functools.partial(_rwkv_kernel, t_valid=t_valid, precise=precise),
        grid=(n_seq // n_sb, n_chunks),
        in_specs=[pl.BlockSpec((n_sb, WKV_CHUNK, N_SHIFT), lambda s, c: (s, c, 0)),
                  pl.BlockSpec((n_sb, 1, N_SHIFT), lambda s, c: (s, 0, 0)),
                  state_spec,
                  rowvec(N_SHIFT), rowvec(D_RWKV), full(mix["wcomb"]), rowvec(D_RWKV), full(mix["wgate"]),
                  rowvec(D_RWKV), rowvec(D_RWKV), rowvec(D_RWKV), rowvec(D_RWKV), rowvec(D_RWKV), full(e128)],
        out_specs=[pl.BlockSpec((n_sb, WKV_CHUNK, D_RWKV), lambda s, c: (s, c, 0)), state_spec],
        out_shape=[jax.ShapeDtypeStruct((n_seq, t_pad, D_RWKV), F32),
                   jax.ShapeDtypeStruct((n_seq, N_PAIRS, V7X_LANES, V7X_LANES), F32)],
        scratch_shapes=[pltpu.VMEM((n_sb, N_PAIRS, V7X_LANES, V7X_LANES), F32),
                        pltpu.VMEM((n_sb, 1, N_SHIFT), F32)],
        compiler_params=_params("parallel", "arbitrary"),
        name="rwkv_mix",
    )(zr, shift0.reshape(n_seq, 1, N_SHIFT), s0_bd, mix["mu"], mix["w0"], mix["wcomb"], mix["a0"], mix["wgate"],
      mix["k_k"], mix["k_a"], mix["r_k"], mix["ln_w"], mix["ln_b"], e128)


def _state_to_block_diag(s):
    n = s.shape[0]
    s = s.reshape(n, N_PAIRS, 2, HD_RWKV, HD_RWKV)
    eye = jnp.eye(2, dtype=s.dtype)
    bd = jnp.einsum("nphvk,hg->nphvgk", s, eye)
    return bd.reshape(n, N_PAIRS, V7X_LANES, V7X_LANES)


def _state_from_block_diag(bd):
    n = bd.shape[0]
    bd = bd.reshape(n, N_PAIRS, 2, HD_RWKV, 2, HD_RWKV)
    s = jnp.stack([bd[:, :, 0, :, 0, :], bd[:, :, 1, :, 1, :]], axis=2)
    return s.reshape(n, H_RWKV, HD_RWKV, HD_RWKV)


def _pool_kernel(u_ref, hist_ref, w_ref, scale_ref, y_ref, carry_scr, *, pos0, precise):
    t = pl.program_id(1)
    tm = u_ref.shape[0]

    @pl.when(t == 0)
    def _():
        carry_scr[...] = hist_ref[0]

    u = u_ref[...]
    ext = jnp.concatenate([carry_scr[...], u], axis=0)
    carry_scr[...] = ext[tm:, :]
    pos = pos0 + t * tm + lax.broadcasted_iota(jnp.int32, (tm, 1), 0)
    acc = ext
    sums = []
    for win in POOL_WINDOWS:
        acc = acc + pltpu.roll(acc, win // 2, 0)
        sums.append(acc[POOL_CARRY:, :])
    for gi, win in enumerate(POOL_WINDOWS):
        sl = slice(gi * POOL_GROUP, (gi + 1) * POOL_GROUP)
        count = jnp.minimum(pos + 1, win).astype(F32)
        x = sums[gi][:, sl] / count - u[:, sl]
        y_ref[:, sl] = _mm(x, w_ref[gi], precise=precise) * scale_ref[:, sl]


def _pool_mix(u, n_seq, t_len, hist, pos0, pool_w_bf16, pool_scale, precise):
    tm = min(512, t_len)
    assert t_len % tm == 0
    n_t = t_len // tm
    return pl.pallas_call(
        functools.partial(_pool_kernel, pos0=pos0, precise=precise),
        grid=(n_seq, n_t),
        in_specs=[pl.BlockSpec((tm, D_POOL), lambda s, t: (s * n_t + t, 0)),
                  pl.BlockSpec((1, POOL_CARRY, D_POOL), lambda s, t: (s, 0, 0)),
                  pl.BlockSpec(pool_w_bf16.shape, lambda s, t: (0, 0, 0)),
                  pl.BlockSpec((1, D_POOL), lambda s, t: (0, 0))],
        out_specs=pl.BlockSpec((tm, D_POOL), lambda s, t: (s * n_t + t, 0)),
        out_shape=jax.ShapeDtypeStruct((n_seq * t_len, D_POOL), F32),
        scratch_shapes=[pltpu.VMEM((POOL_CARRY, D_POOL), F32)],
        compiler_params=_params("parallel", "arbitrary"),
        name="pool_mix",
    )(u, hist, pool_w_bf16, pool_scale.reshape(1, D_POOL))


def _sub_norm(acc, l, lam, subw, lam_init):
    tq = acc.shape[0] // 2
    o = acc[:tq] / l[:tq] - lam * (acc[tq:] / l[tq:])
    return o * lax.rsqrt(jnp.mean(o * o, axis=-1, keepdims=True) + RMS_EPS) * subw * (1.0 - lam_init)


def _attn_prompt_kernel(lam_ref, q_ref, k_ref, v_ref, subw_ref, o_ref, kb_scr, va_scr, m_scr, acc_scr, *,
                        lam_init, strip):
    tq = q_ref.shape[1]
    qi = pl.program_id(2)
    lane = lax.broadcasted_iota(jnp.int32, (1, D_HEAD), 1)

    @pl.when(qi == 0)
    def _():
        kb_scr[...] = k_ref[0].astype(BF16)
        va_scr[:, :D_HEAD] = v_ref[0].astype(BF16)
        va_scr[:, D_HEAD:] = jnp.ones((va_scr.shape[0], D_HEAD), BF16)

    q = q_ref[0] * (HD_ATT ** -0.5)
    qs = jnp.concatenate([jnp.where(lane < HD_ATT, q, 0.0), jnp.where(lane < HD_ATT, 0.0, q)], axis=0).astype(BF16)
    m_scr[...] = jnp.full(m_scr.shape, NEG_INF, F32)
    acc_scr[...] = jnp.zeros(acc_scr.shape, F32)

    strips = [slice(r, r + strip) for r in range(0, 2 * tq, strip)]

    def process(blocks):
        starts = [pl.multiple_of(j * tq, tq) for j, _ in blocks]
        items = [(masked, st, rows, (rows.start % tq + strip) if masked else tq)
                 for (_, masked), st in zip(blocks, starts) for rows in strips]
        scores = [lax.dot_general(qs[rows], kb_scr[pl.ds(st, n_keys), :], _NT, preferred_element_type=F32)
                  for _, st, rows, n_keys in items]
        for (masked, st, rows, n_keys), s in zip(items, scores):
            if masked:
                qpos = (rows.start % tq) + lax.broadcasted_iota(jnp.int32, (strip, 1), 0)
                kpos = lax.broadcasted_iota(jnp.int32, (1, n_keys), 1)
                s = jnp.where(kpos <= qpos, s, NEG_INF)
            m_old = m_scr[rows, :]
            m_new = jnp.maximum(m_old, jnp.max(s, axis=-1, keepdims=True))
            alpha = jnp.exp(m_old - m_new)
            p = jnp.concatenate([jnp.exp(s[:, c * V7X_LANES:(c + 1) * V7X_LANES] - m_new)
                                 for c in range(n_keys // V7X_LANES)], axis=1).astype(BF16)
            pv = jnp.dot(p, va_scr[pl.ds(st, n_keys), :], preferred_element_type=F32)
            acc_scr[rows, :] = jnp.concatenate([alpha, alpha], axis=1) * acc_scr[rows, :] + pv
            m_scr[rows, :] = m_new

    def body(jj, carry):
        process([(ATTN_BLOCKS * jj + b, False) for b in range(ATTN_BLOCKS)])
        return carry

    lax.fori_loop(0, qi // ATTN_BLOCKS, body, 0)
    for rest in range(ATTN_BLOCKS):
        @pl.when(qi % ATTN_BLOCKS == rest)
        def _(rest=rest):
            process([(qi - rest + b, False) for b in range(rest)] + [(qi, True)])

    acc = acc_scr[...]
    o_ref[0] = _sub_norm(acc[:, :D_HEAD], acc[:, D_HEAD:], lam_ref[0, 0], subw_ref[...], lam_init)


def _attn_prompt(q, k, v, lam, subw, lam_init, n_batch, t_len):
    tq = min(512, t_len)
    strip = min(ATTN_STRIP, tq)
    assert t_len % tq == 0
    kv_spec = pl.BlockSpec((1, t_len, D_HEAD), lambda b, h, i: (b, 0, h))
    return pl.pallas_call(
        functools.partial(_attn_prompt_kernel, lam_init=lam_init, strip=strip),
        grid=(n_batch, H_ATT, t_len // tq),
        in_specs=[pl.BlockSpec(memory_space=pltpu.SMEM),
                  pl.BlockSpec((1, tq, D_HEAD), lambda b, h, i: (b, i, h)),
                  kv_spec, kv_spec,
                  pl.BlockSpec((1, D_HEAD), lambda b, h, i: (0, 0))],
        out_specs=pl.BlockSpec((1, tq, D_HEAD), lambda b, h, i: (b, i, h)),
        out_shape=jax.ShapeDtypeStruct((n_batch, t_len, D_MODEL), F32),
        scratch_shapes=[pltpu.VMEM((t_len, D_HEAD), BF16), pltpu.VMEM((t_len, 2 * D_HEAD), BF16),
                        pltpu.VMEM((2 * tq, V7X_LANES), F32), pltpu.VMEM((2 * tq, 2 * D_HEAD), F32)],
        compiler_params=_params("parallel", "parallel", "arbitrary"),
        name="attn_prompt",
    )(lam.reshape(1, 1), q, k, v, subw.reshape(1, D_HEAD))


def _attn_sample_kernel(pt_ref, lam_ref, q_ref, kn_ref, vn_ref, subw_ref, *refs, lam_init, t_new):
    k_refs = refs[:PAGES_PER_STEP]
    v_refs = refs[PAGES_PER_STEP:2 * PAGES_PER_STEP]
    o_ref, qm_scr, bias_scr, m_scr, l_scr, acc_scr = refs[2 * PAGES_PER_STEP:]
    g = pl.program_id(1)
    n_rows = 2 * H_ATT * t_new
    n_cols = PAGE_SIZE * H_ATT
    row_head = lax.broadcasted_iota(jnp.int32, (n_rows, 1), 0) // (2 * t_new)
    lane = lax.broadcasted_iota(jnp.int32, (1, D_HEAD), 1)

    @pl.when(g == 0)
    def _():
        q = q_ref[0] * (HD_ATT ** -0.5)
        pieces = []
        for h in range(H_ATT):
            qh = q[:, h * D_HEAD:(h + 1) * D_HEAD]
            pieces += [jnp.where(lane < HD_ATT, qh, 0.0), jnp.where(lane < HD_ATT, 0.0, qh)]
        qm_scr[...] = jnp.concatenate(pieces, axis=0).astype(BF16)
        col_head = lax.broadcasted_iota(jnp.int32, (1, n_cols), 1) % H_ATT
        bias_scr[...] = jnp.where(row_head == col_head, 0.0, NEG_INF)
        m_scr[...] = jnp.full(m_scr.shape, NEG_INF, F32)
        l_scr[...] = jnp.zeros(l_scr.shape, F32)
        acc_scr[...] = jnp.zeros(acc_scr.shape, F32)

    def update(scores, values):
        m, l, acc = m_scr[...], l_scr[...], acc_scr[...]
        for s, vmat in zip(scores, values):
            m_new = jnp.maximum(m, jnp.max(s, axis=-1, keepdims=True))
            alpha = jnp.exp(m - m_new)
            p = jnp.concatenate([jnp.exp(s[:, c * V7X_LANES:(c + 1) * V7X_LANES] - m_new)
                                 for c in range(s.shape[1] // V7X_LANES)], axis=1)
            l = alpha * l + jnp.sum(p, axis=-1, keepdims=True)
            acc = alpha * acc + jnp.dot(p.astype(BF16), vmat, preferred_element_type=F32)
            m = m_new
        m_scr[...], l_scr[...], acc_scr[...] = m, l, acc

    qm = qm_scr[...]
    keys = [k_refs[i][0, 0].reshape(n_cols, D_HEAD).astype(BF16) for i in range(PAGES_PER_STEP)]
    values = [v_refs[i][0, 0].reshape(n_cols, D_HEAD).astype(BF16) for i in range(PAGES_PER_STEP)]
    update([lax.dot_general(qm, kf, _NT, preferred_element_type=F32) + bias_scr[...] for kf in keys], values)

    @pl.when(g == pl.num_programs(1) - 1)
    def _():
        n_new = t_new * H_ATT
        pad = jnp.zeros((V7X_LANES - n_new, D_HEAD), BF16)
        kn = jnp.concatenate([kn_ref[0].reshape(n_new, D_HEAD).astype(BF16), pad], axis=0)
        vn = jnp.concatenate([vn_ref[0].reshape(n_new, D_HEAD).astype(BF16), pad], axis=0)
        col = lax.broadcasted_iota(jnp.int32, (1, V7X_LANES), 1)
        qpos = lax.broadcasted_iota(jnp.int32, (n_rows, 1), 0) % t_new
        ok = (col < n_new) & (col % H_ATT == row_head) & (col // H_ATT <= qpos)
        update([jnp.where(ok, lax.dot_general(qm, kn, _NT, preferred_element_type=F32), NEG_INF)], [vn])
        acc = acc_scr[...]
        l = l_scr[...]
        lam = lam_ref[0, 0]
        for h in range(H_ATT):
            rows = slice(2 * h * t_new, 2 * (h + 1) * t_new)
            o_ref[0, :, h * D_HEAD:(h + 1) * D_HEAD] = _sub_norm(acc[rows], l[rows], lam, subw_ref[...], lam_init)


def _attn_sample(q, k_new, v_new, cache_k, cache_v, layer_j, page_table, lam, subw, lam_init):
    n_batch, t_new, _ = q.shape
    n_pages = page_table.shape[1]
    assert n_pages % PAGES_PER_STEP == 0 and t_new <= PAGE_SIZE and t_new % V7X_SUBLANES == 0
    n_steps = n_pages // PAGES_PER_STEP
    n_rows = 2 * H_ATT * t_new
    assert n_rows == V7X_LANES and t_new * H_ATT <= V7X_LANES

    def page_spec(i):
        return pl.BlockSpec((1, 1, PAGE_SIZE, H_ATT, D_HEAD),
                            lambda b, g, pt: (layer_j, pt[b, g * PAGES_PER_STEP + i], 0, 0, 0))

    tok_spec = pl.BlockSpec((1, t_new, D_MODEL), lambda b, g, pt: (b, 0, 0))
    new_spec = pl.BlockSpec((1, t_new, H_ATT, D_HEAD), lambda b, g, pt: (b, 0, 0, 0))
    grid_spec = pltpu.PrefetchScalarGridSpec(
        num_scalar_prefetch=1,
        grid=(n_batch, n_steps),
        in_specs=[pl.BlockSpec(memory_space=pltpu.SMEM), tok_spec, new_spec, new_spec,
                  pl.BlockSpec((1, D_HEAD), lambda b, g, pt: (0, 0))]
                 + [page_spec(i) for i in range(PAGES_PER_STEP)] * 2,
        out_specs=tok_spec,
        scratch_shapes=[pltpu.VMEM((n_rows, D_HEAD), BF16),
                        pltpu.VMEM((n_rows, PAGE_SIZE * H_ATT), F32),
                        pltpu.VMEM((n_rows, V7X_LANES), F32), pltpu.VMEM((n_rows, V7X_LANES), F32),
                        pltpu.VMEM((n_rows, D_HEAD), F32)])
    new_shape = (n_batch, t_new, H_ATT, D_HEAD)
    return pl.pallas_call(
        functools.partial(_attn_sample_kernel, lam_init=lam_init, t_new=t_new),
        grid_spec=grid_spec,
        out_shape=jax.ShapeDtypeStruct((n_batch, t_new, D_MODEL), F32),
        compiler_params=_params("parallel", "arbitrary"),
        name="attn_sample",
    )(page_table, lam.reshape(1, 1), q, k_new.reshape(new_shape), v_new.reshape(new_shape), subw.reshape(1, D_HEAD),
      *([cache_k] * PAGES_PER_STEP), *([cache_v] * PAGES_PER_STEP))


ROUTE_LANES = ("expert0", "expert1", "gate0", "gate1", "rank0", "rank1")


def _router_kernel(h_ref, w_ref, b_ref, o_ref, cnt_ref, carry_scr):
    i = pl.program_id(0)
    tm = h_ref.shape[0]

    @pl.when(i == 0)
    def _():
        carry_scr[...] = jnp.zeros(carry_scr.shape, F32)

    logits = _mm(h_ref[...], w_ref[...], precise=True) + b_ref[...]
    lane = lax.broadcasted_iota(jnp.int32, (1, V7X_LANES), 1)
    lane_f = lane.astype(F32)

    def max_first(mask):
        v = jnp.where(mask, logits, -jnp.inf)
        m = jnp.max(v, axis=-1, keepdims=True)
        return m, jnp.min(jnp.where(v == m, lane_f, float(V7X_LANES)), axis=-1, keepdims=True)

    is_group = lane < N_GROUPS
    m_g, grp = max_first(is_group)
    g_gate = 1.0 / jnp.sum(jnp.where(is_group, jnp.exp(logits - m_g), 0.0), axis=-1, keepdims=True)
    first = N_GROUPS + EXPERTS_PER_GROUP * grp
    in_group = (lane_f >= first) & (lane_f < first + EXPERTS_PER_GROUP)
    v1, l1 = max_first(in_group)
    v2, l2 = max_first(in_group & (lane_f != l1))
    e21 = jnp.exp(v2 - v1)
    gate0 = g_gate / (1.0 + e21)
    gate1 = g_gate * e21 / (1.0 + e21)

    hot0 = lane_f == l1
    hot1 = lane_f == l2
    cnt = hot0.astype(F32) + hot1.astype(F32)
    earlier = lax.broadcasted_iota(jnp.int32, (tm, tm), 0) > lax.broadcasted_iota(jnp.int32, (tm, tm), 1)
    before = carry_scr[...] + jnp.dot(earlier.astype(BF16), cnt.astype(BF16), preferred_element_type=F32)
    rank0 = jnp.sum(jnp.where(hot0, before, 0.0), axis=-1, keepdims=True)
    rank1 = jnp.sum(jnp.where(hot1, before, 0.0), axis=-1, keepdims=True)
    carry_scr[...] = carry_scr[...] + jnp.sum(cnt, axis=0, keepdims=True)
    cnt_ref[...] = carry_scr[...]

    cols = dict(expert0=l1 - N_GROUPS, expert1=l2 - N_GROUPS, gate0=gate0, gate1=gate1, rank0=rank0, rank1=rank1)
    out = jnp.zeros((tm, V7X_LANES), F32)
    for k, name in enumerate(ROUTE_LANES):
        out = jnp.where(lane == k, cols[name], out)
    o_ref[...] = out


def _route(h, w_pad, b_pad):
    n, d = h.shape
    tm = min(512, n)
    assert n % tm == 0
    return pl.pallas_call(
        _router_kernel,
        grid=(n // tm,),
        in_specs=[pl.BlockSpec((tm, d), lambda i: (i, 0)),
                  pl.BlockSpec(w_pad.shape, lambda i: (0, 0)),
                  pl.BlockSpec((1, V7X_LANES), lambda i: (0, 0))],
        out_specs=[pl.BlockSpec((tm, V7X_LANES), lambda i: (i, 0)), pl.BlockSpec((1, V7X_LANES), lambda i: (0, 0))],
        out_shape=[jax.ShapeDtypeStruct((n, V7X_LANES), F32), jax.ShapeDtypeStruct((1, V7X_LANES), F32)],
        scratch_shapes=[pltpu.VMEM((1, V7X_LANES), F32)],
        compiler_params=_params("arbitrary"),
        name="route",
    )(h, w_pad, b_pad)


def _row_copy(src, src_row, dst, dst_row, sem):
    return pltpu.make_async_copy(src.at[pl.ds(src_row, 1), :], dst.at[pl.ds(dst_row, 1), :], sem)


def _scatter_rows_kernel(dest_ref, h_ref, xg_init_ref, xg_ref, sem):
    del xg_init_ref
    tm = h_ref.shape[0]

    def start(t, carry):
        for s in range(TOP_K):
            _row_copy(h_ref, t, xg_ref, dest_ref[0, 0, TOP_K * t + s], sem).start(priority=s)
        return carry

    def wait(a, carry):
        _row_copy(h_ref, 0, xg_ref, 0, sem).wait()
        return carry

    lax.fori_loop(0, tm, start, 0, unroll=4)
    lax.fori_loop(0, TOP_K * tm, wait, 0, unroll=8)


def _scatter_rows(h, dest, n_rows):
    n, d = h.shape
    tm = min(256, n)
    assert n % tm == 0 and TOP_K == 2
    return pl.pallas_call(
        _scatter_rows_kernel,
        grid=(n // tm,),
        in_specs=[pl.BlockSpec((1, 1, TOP_K * tm), lambda i: (i, 0, 0), memory_space=pltpu.SMEM),
                  pl.BlockSpec((tm, d), lambda i: (i, 0)),
                  pl.BlockSpec(memory_space=pl.ANY)],
        out_specs=pl.BlockSpec(memory_space=pl.ANY),
        out_shape=jax.ShapeDtypeStruct((n_rows, d), F32),
        scratch_shapes=[pltpu.SemaphoreType.DMA(())],
        input_output_aliases={2: 0},
        compiler_params=_params("arbitrary"),
        name="scatter_rows",
    )(dest.reshape(n // tm, 1, TOP_K * tm), h, jnp.zeros((n_rows, d), F32))


def _combine_kernel(dest_ref, dest_next_ref, x_ref, gates_ref, g_ref, y_hbm_ref, o_ref, ybuf, sem, *, normed):
    i = pl.program_id(0)
    tm = x_ref.shape[0]
    slot = i % 2

    def fetch(d_ref, s):
        def start(t, carry):
            for k in range(TOP_K):
                _row_copy(y_hbm_ref, d_ref[0, 0, TOP_K * t + k], ybuf.at[s, k], t, sem.at[s]).start(priority=k)
            return carry
        lax.fori_loop(0, tm, start, 0, unroll=4)

    @pl.when(i == 0)
    def _():
        fetch(dest_ref, 0)

    @pl.when(i + 1 < pl.num_programs(0))
    def _():
        fetch(dest_next_ref, 1 - slot)

    def wait(a, carry):
        _row_copy(y_hbm_ref, 0, ybuf.at[slot, 0], 0, sem.at[slot]).wait()
        return carry

    lax.fori_loop(0, TOP_K * tm, wait, 0, unroll=8)
    gates = gates_ref[...]
    x = x_ref[...] + (ybuf[slot, 0] * gates[:, 0:1] + ybuf[slot, 1] * gates[:, 1:2])
    o_ref[...] = _rms(x, g_ref[...]) if normed else x


def _combine(x, y_rows, dest, gates, g, normed):
    n, d = x.shape
    tm = min(256, n)
    assert n % tm == 0 and TOP_K == 2
    n_tiles = n // tm
    row = pl.BlockSpec((tm, d), lambda i: (i, 0))
    dest_tiles = dest.reshape(n_tiles, 1, TOP_K * tm)
    return pl.pallas_call(
        functools.partial(_combine_kernel, normed=normed),
        grid=(n_tiles,),
        in_specs=[pl.BlockSpec((1, 1, TOP_K * tm), lambda i: (i, 0, 0), memory_space=pltpu.SMEM),
                  pl.BlockSpec((1, 1, TOP_K * tm), lambda i: (jnp.minimum(i + 1, n_tiles - 1), 0, 0),
                               memory_space=pltpu.SMEM),
                  row, pl.BlockSpec((tm, TOP_K), lambda i: (i, 0)), pl.BlockSpec((1, d), lambda i: (0, 0)),
                  pl.BlockSpec(memory_space=pl.ANY)],
        out_specs=row,
        out_shape=jax.ShapeDtypeStruct((n, d), F32),
        scratch_shapes=[pltpu.VMEM((2, TOP_K, tm, d), F32), pltpu.SemaphoreType.DMA((2,))],
        compiler_params=_params("arbitrary"),
        name="combine",
    )(dest_tiles, dest_tiles, x, gates, g.reshape(1, d), y_rows)


def _expert_kernel(be_ref, bv_ref, x_ref, wg_ref, wu_ref, wd_ref, y_ref, wg_scr, wu_scr, wd_scr):
    i = pl.program_id(0)

    @pl.when((i == 0) | (be_ref[i] != be_ref[jnp.maximum(i - 1, 0)]))
    def _():
        wg_scr[...] = wg_ref[0, 0].astype(BF16)
        wu_scr[...] = wu_ref[0, 0].astype(BF16)
        wd_scr[...] = wd_ref[0, 0].astype(BF16)

    @pl.when(bv_ref[i] > 0)
    def _():
        x = x_ref[...].astype(BF16)
        hg = jnp.dot(x, wg_scr[...], preferred_element_type=F32)
        hu = jnp.dot(x, wu_scr[...], preferred_element_type=F32)
        hdn = (hg * _sigmoid(hg)) * hu
        y_ref[...] = jnp.dot(hdn.astype(BF16), wd_scr[...], preferred_element_type=F32)

    @pl.when(bv_ref[i] == 0)
    def _():
        y_ref[...] = jnp.zeros(y_ref.shape, F32)


def _expert_blocks(xg, block_expert, block_valid, wg, wu, wd, layer):
    n_rows, d = xg.shape
    n_blocks = n_rows // MOE_ROWS
    grid_spec = pltpu.PrefetchScalarGridSpec(
        num_scalar_prefetch=2,
        grid=(n_blocks,),
        in_specs=[pl.BlockSpec((MOE_ROWS, d), lambda i, be, bv: (i, 0)),
                  pl.BlockSpec((1, 1, d, D_EXPERT), lambda i, be, bv: (layer, be[i], 0, 0)),
                  pl.BlockSpec((1, 1, d, D_EXPERT), lambda i, be, bv: (layer, be[i], 0, 0)),
                  pl.BlockSpec((1, 1, D_EXPERT, d), lambda i, be, bv: (layer, be[i], 0, 0))],
        out_specs=pl.BlockSpec((MOE_ROWS, d), lambda i, be, bv: (i, 0)),
        scratch_shapes=[pltpu.VMEM((d, D_EXPERT), BF16), pltpu.VMEM((d, D_EXPERT), BF16),
                        pltpu.VMEM((D_EXPERT, d), BF16)])
    return pl.pallas_call(
        _expert_kernel,
        grid_spec=grid_spec,
        out_shape=jax.ShapeDtypeStruct((n_rows, d), F32),
        compiler_params=_params("arbitrary"),
        name="expert_blocks",
    )(block_expert, block_valid, xg, wg, wu, wd)


def _hier_moe(h, moe):
    n, d = h.shape
    routed, lane_counts = _route(h, moe["w_router"], moe["b_router"])
    col = {name: routed[:, k] for k, name in enumerate(ROUTE_LANES)}
    eidx = jnp.stack([col["expert0"], col["expert1"]], axis=1).astype(jnp.int32)
    rank = jnp.stack([col["rank0"], col["rank1"]], axis=1).astype(jnp.int32)
    gates = jnp.stack([col["gate0"], col["gate1"]], axis=1)
    counts = lane_counts[0, N_GROUPS:N_GROUPS + N_EXPERTS].astype(jnp.int32)

    padded = (counts + MOE_ROWS - 1) // MOE_ROWS * MOE_ROWS
    pad_end = jnp.cumsum(padded)
    pad_start = pad_end - padded
    experts = jnp.arange(N_EXPERTS, dtype=jnp.int32)
    dest = jnp.sum(jnp.where(eidx[:, :, None] == experts, pad_start, 0), axis=-1) + rank
    n_blocks = -(-n * TOP_K // MOE_ROWS) + N_EXPERTS
    block_start = jnp.arange(n_blocks, dtype=jnp.int32) * MOE_ROWS
    block_expert = jnp.sum((block_start[:, None] >= pad_end[None, :]).astype(jnp.int32), axis=1)
    block_expert = jnp.minimum(block_expert, N_EXPERTS - 1)
    block_valid = (block_start < pad_end[-1]).astype(jnp.int32)
    xg = _scatter_rows(h, dest, n_blocks * MOE_ROWS)
    y_rows = _expert_blocks(xg, block_expert, block_valid, moe["wg"], moe["wu"], moe["wd"], moe["layer"])
    return y_rows, dest, gates


def _mix_params(j, precise, w_in_mix, mu_shift, w_decay0, w_decay_up, w_iclr0, w_iclr_up, w_gate_up,
                k_k, k_a, r_k, ln_x_w, ln_x_b, pool_w, pool_scale, w_out_mix):
    wdt = F32 if precise else BF16
    zero = jnp.zeros((DECAY_LORA, D_RWKV), F32)
    vec = lambda a: a[j].reshape(1, -1)
    return dict(
        w_in=w_in_mix[j].astype(wdt), mu=vec(mu_shift), w0=vec(w_decay0),
        wcomb=jnp.block([[w_decay_up[j], zero], [zero, w_iclr_up[j]]]).astype(wdt),
        a0=vec(w_iclr0), wgate=w_gate_up[j].astype(wdt), k_k=vec(k_k), k_a=vec(k_a), r_k=vec(r_k),
        ln_w=vec(ln_x_w), ln_b=vec(ln_x_b), pool_w=pool_w[j].astype(wdt), pool_scale=pool_scale[j],
        w_out_a=w_out_mix[j, :D_RWKV].astype(wdt), w_out_b=w_out_mix[j, D_RWKV:].astype(wdt))


def _mixer_group(x, n_seq, t_len, pos0, shift0, wkv0, pool0, mix, norm_g, norm_ffn_g, precise):
    zr, u = _norm_matmul(x, norm_g, mix["w_in"], (N_SHIFT, D_POOL), precise)
    zr = zr.reshape(n_seq, t_len, N_SHIFT)
    t_pad = -(-t_len // WKV_CHUNK) * WKV_CHUNK
    zr_pad = jnp.pad(zr, ((0, 0), (0, t_pad - t_len), (0, 0))) if t_pad != t_len else zr
    y_rwkv, s_bd = _rwkv_mix(zr_pad, n_seq, t_pad, min(t_len, WKV_CHUNK), shift0, _state_to_block_diag(wkv0),
                             mix, precise)
    y_rwkv = y_rwkv[:, :t_len].reshape(n_seq * t_len, D_RWKV)
    hist = jnp.pad(pool0, ((0, 0), (POOL_CARRY - POOL_BUF, 0), (0, 0)))
    y_pool = _pool_mix(u, n_seq, t_len, hist, pos0, mix["pool_w"], mix["pool_scale"], precise)
    x, h = _matmul_res(x, [(y_rwkv, mix["w_out_a"]), (y_pool, mix["w_out_b"])], norm_ffn_g, precise)
    ext = jnp.concatenate([pool0, u.reshape(n_seq, t_len, D_POOL)], axis=1)
    return x, h, zr[:, -1], _state_from_block_diag(s_bd), ext[:, -POOL_BUF:]


def kernel(x_prompt, x_sample, state_wkv, state_shift, state_pool, cache_k, cache_v, page_table,
           norm_mix, norm_ffn, norm_final,
           w_in_mix, mu_shift, w_decay0, w_decay_up, w_iclr0, w_iclr_up, w_gate_up,
           k_k, k_a, r_k, ln_x_w, ln_x_b, pool_w, pool_scale, w_out_mix,
           w_qkv, lam_q1, lam_k1, lam_q2, lam_k2, subln_w, w_o,
           router_group, router_group_bias, router_expert, router_expert_bias,
           w_exp_gate, w_exp_up, w_exp_down):
    bp, tp, d = x_prompt.shape
    bs, ts, _ = x_sample.shape
    depth = norm_mix.shape[0]
    past_len = page_table.shape[1] * PAGE_SIZE
    groups = [dict(x=x_prompt.reshape(bp * tp, d), n_seq=bp, t=tp, precise=False),
              dict(x=x_sample.reshape(bs * ts, d), n_seq=bs, t=ts, precise=True)]
    outs = [dict(wkv=[], shift=[], pool=[], k=[], v=[]) for _ in groups]

    for layer in range(depth):
        j = layer // 2
        if layer % 2 == 0:
            mix_w = (w_in_mix, mu_shift, w_decay0, w_decay_up, w_iclr0, w_iclr_up, w_gate_up,
                     k_k, k_a, r_k, ln_x_w, ln_x_b, pool_w, pool_scale, w_out_mix)
            inits = [(0, jnp.zeros((bp, N_SHIFT), F32), jnp.zeros((bp, H_RWKV, HD_RWKV, HD_RWKV), F32),
                      jnp.zeros((bp, POOL_BUF, D_POOL), F32)),
                     (past_len, state_shift[j], state_wkv[j], state_pool[j])]
            for grp, out, (pos0, sh0, wk0, pl0) in zip(groups, outs, inits):
                mix = _mix_params(j, grp["precise"], *mix_w)
                grp["x"], grp["h"], sh, wk, po = _mixer_group(grp["x"], grp["n_seq"], grp["t"], pos0, sh0, wk0, pl0,
                                                              mix, norm_mix[layer], norm_ffn[layer], grp["precise"])
                out["wkv"].append(wk); out["shift"].append(sh); out["pool"].append(po)
        else:
            lam_init = 0.8 - 0.6 * math.exp(-0.3 * layer)
            lam = (jnp.exp(jnp.sum(lam_q1[j] * lam_k1[j])) - jnp.exp(jnp.sum(lam_q2[j] * lam_k2[j])) + lam_init)
            for gi, (grp, out) in enumerate(zip(groups, outs)):
                n_seq, t, precise = grp["n_seq"], grp["t"], grp["precise"]
                wdt = F32 if precise else BF16
                q, k, v = _norm_matmul(grp["x"], norm_mix[layer], w_qkv[j].astype(wdt), (d, d, d), precise)
                q3, k3, v3 = (a.reshape(n_seq, t, d) for a in (q, k, v))
                if gi == 0:
                    o = _attn_prompt(q3, k3, v3, lam, subln_w[j], lam_init, n_seq, t)
                else:
                    o = _attn_sample(q3, k3, v3, cache_k, cache_v, j, page_table, lam, subln_w[j], lam_init)
                grp["x"], grp["h"] = _matmul_res(grp["x"], [(o.reshape(n_seq * t, d), w_o[j].astype(wdt))],
                                                 norm_ffn[layer], precise)
                out["k"].append(k.reshape(n_seq, t, H_ATT, D_HEAD)); out["v"].append(v.reshape(n_seq, t, H_ATT, D_HEAD))

        n_router = N_GROUPS + N_EXPERTS
        moe = dict(
            w_router=jnp.pad(jnp.concatenate([router_group[layer], router_expert[layer]], axis=1),
                             ((0, 0), (0, V7X_LANES - n_router))),
            b_router=jnp.pad(jnp.concatenate([router_group_bias[layer], router_expert_bias[layer]]),
                             (0, V7X_LANES - n_router)).reshape(1, V7X_LANES),
            wg=w_exp_gate, wu=w_exp_up, wd=w_exp_down, layer=layer)
        for grp in groups:
            y_rows, dest, gates = _hier_moe(grp["h"], moe)
            grp["x"] = _combine(grp["x"], y_rows, dest, gates, norm_final, normed=(layer == depth - 1))

    res = []
    for grp, out in zip(groups, outs):
        res.append([grp["x"].reshape(grp["n_seq"], grp["t"], d), jnp.stack(out["wkv"]), jnp.stack(out["shift"]),
                    jnp.stack(out["pool"]), jnp.stack(out["k"]), jnp.stack(out["v"])])
    (yp, wkp, shp, pop, kp, vp), (ys, wks, shs, pos_, ks_, vs_) = res
    return (yp, ys, wkp, shp, pop, kp, vp, wks, shs, pos_, ks_, vs_)
```
